```python
import math
import jax, jax.numpy as jnp
from jax import lax
import numpy as np

D_MODEL = 1024
BATCH = 8
SEQ = 4096
DEPTH = 2

CHUNK = 64
NORM_EPS = 1e-6

RWKV_HEAD_SIZE = 64
RWKV_WIDTH = D_MODEL
RWKV_HEADS = RWKV_WIDTH // RWKV_HEAD_SIZE
DECAY_LORA = 64
ICLR_LORA = 64
VRES_LORA = 32
GATE_LORA = 160
RWKV_GN_EPS = 64e-5

SSD_WIDTH = 2 * D_MODEL
SSD_HEAD_DIM = 64
SSD_HEADS = SSD_WIDTH // SSD_HEAD_DIM
SSD_STATE = 128
SSD_GROUPS = 4
SSD_CONV = 4
SSD_CONV_CH = SSD_WIDTH + 2 * SSD_GROUPS * SSD_STATE

PEER_HEADS = 8
PEER_NKEYS = 128
PEER_EXPERTS = PEER_NKEYS * PEER_NKEYS
PEER_TOPK = 16
PEER_QDIM = 256
PEER_TOKEN_BLOCK = 128

RWKV_SPLITS = (RWKV_WIDTH, RWKV_WIDTH, RWKV_WIDTH, DECAY_LORA, ICLR_LORA, GATE_LORA)
RWKV_PROJ = sum(RWKV_SPLITS)
SSD_SPLITS = (SSD_WIDTH, SSD_CONV_CH, SSD_HEADS)
SSD_PROJ = sum(SSD_SPLITS)
GATE_PROJ = 2 * D_MODEL
IN_PROJ = RWKV_PROJ + SSD_PROJ + GATE_PROJ

kernel_name = 'hybrid_rwkv7_ssd_peer_block'


def rmsnorm(x, g):
    xf = x.astype(jnp.float32)
    y = xf * lax.rsqrt(jnp.mean(xf * xf, axis=-1, keepdims=True) + NORM_EPS)
    return (y * g.astype(jnp.float32)).astype(x.dtype)


def split_last(t, sizes):
    out, start = [], 0
    for s in sizes:
        out.append(t[..., start:start + s])
        start += s
    return out


def shift_prev(x):
    return jnp.pad(x, ((0, 0), (1, 0), (0, 0)))[:, :-1]


def causal_depthwise_conv(x, w, b):
    T = x.shape[1]
    xp = jnp.pad(x, ((0, 0), (SSD_CONV - 1, 0), (0, 0)))
    y = b
    for j in range(SSD_CONV):
        y = y + xp[:, j:j + T] * w[j]
    return y


def rwkv7_recurrence(r, w, k, v, a, b):
    Bsz, T, H, N = r.shape

    def step(S, inp):
        r_t, w_t, k_t, v_t, a_t, b_t = inp
        sa = jnp.einsum('bhvk,bhk->bhv', S, a_t)
        S = S * w_t[:, :, None, :] + sa[..., None] * b_t[:, :, None, :] + v_t[..., None] * k_t[:, :, None, :]
        return S, jnp.einsum('bhvk,bhk->bhv', S, r_t)

    xs = tuple(jnp.moveaxis(t.astype(jnp.float32), 1, 0) for t in (r, w, k, v, a, b))
    _, ys = lax.scan(step, jnp.zeros((Bsz, H, N, N), jnp.float32), xs)
    return jnp.moveaxis(ys, 0, 1)


def rwkv7_time_mix(p, v_first, vres, mu, w0, w2, a0, a2, g2, k_k, k_a, r_k, ln_w, ln_b, w_branch):
    Bsz, T, _ = p.shape
    H, N = RWKV_HEADS, RWKV_HEAD_SIZE
    p = p + (shift_prev(p) - p) * mu
    r, k, v, wd, ad, gd = split_last(p, RWKV_SPLITS)
    w_log = -jax.nn.softplus(-(w0 + jnp.tanh(wd) @ w2).astype(jnp.float32)) - 0.5
    decay = jnp.exp(-jnp.exp(w_log))
    a = jax.nn.sigmoid(a0 + ad @ a2)
    g = jax.nn.sigmoid(gd) @ g2
    if vres is None:
        v_first = v
    else:
        v0, v1, v2 = vres
        v = v + (v_first - v) * jax.nn.sigmoid(v0 + (v @ v1) @ v2)
    heads = lambda t: t.reshape(Bsz, T, H, N)
    r, k, v, a, decay = heads(r), heads(k), heads(v), heads(a), heads(decay)
    kk = (k * k_k.reshape(H, N)).astype(jnp.float32)
    kk = kk * lax.rsqrt(jnp.sum(kk * kk, axis=-1, keepdims=True) + 1e-12)
    k = k * (1.0 + (a - 1.0) * k_a.reshape(H, N))
    y = rwkv7_recurrence(r, decay, k, v, -kk, kk * a)
    mean = jnp.mean(y, axis=-1, keepdims=True)
    var = jnp.mean(jnp.square(y - mean), axis=-1, keepdims=True)
    y = (y - mean) * lax.rsqrt(var + RWKV_GN_EPS) * ln_w.reshape(H, N) + ln_b.reshape(H, N)
    y = y + jnp.sum(r * k * r_k, axis=-1, keepdims=True) * v
    y = y.reshape(Bsz, T, RWKV_WIDTH).astype(p.dtype) * g
    return y @ w_branch, v_first


def ssd_chunked_scan(X, a, Bm, Cm):
    Bsz, T, H, P = X.shape
    G, N = Bm.shape[2], Bm.shape[3]
    Hg = H // G
    nc = T // CHUNK
    to_chunks = lambda t: jnp.moveaxis(t.reshape(Bsz, nc, CHUNK, *t.shape[2:]), 1, 0)
    Xc = to_chunks(X.reshape(Bsz, T, G, Hg, P))
    ac = to_chunks(a.reshape(Bsz, T, G, Hg))
    Bc, Cc = to_chunks(Bm), to_chunks(Cm)
    causal = jnp.tril(jnp.ones((CHUNK, CHUNK), dtype=bool))[None, :, :, None, None]

    def step(state, inp):
        x_c, a_c, b_c, c_c = inp
        acs = jnp.cumsum(a_c.astype(jnp.float32), axis=1)
        seg = acs[:, :, None] - acs[:, None, :]
        Lmat = jnp.exp(jnp.where(causal, seg, -jnp.inf))
        cb = jnp.einsum('bign,bjgn->bijg', c_c, b_c)
        y_diag = jnp.einsum('bijgh,bjghp->bighp', Lmat * cb[..., None], x_c)
        y_off = jnp.einsum('bign,bghpn->bighp', c_c, state) * jnp.exp(acs)[..., None]
        decay_to_end = jnp.exp(acs[:, -1:] - acs)
        state = state * jnp.exp(acs[:, -1])[..., None, None] + jnp.einsum('bjgn,bjgh,bjghp->bghpn', b_c, decay_to_end, x_c)
        return state, y_diag + y_off

    state0 = jnp.zeros((Bsz, G, Hg, P, N), jnp.float32)
    _, ys = lax.scan(step, state0, (Xc, ac, Bc, Cc))
    return jnp.moveaxis(ys, 0, 1).reshape(Bsz, T, H, P)


def ssd_mixer(p, conv_w, conv_b, dt_bias, A_log, D_skip, norm_g, w_branch):
    Bsz, T, _ = p.shape
    z, xbc, dt_raw = split_last(p, SSD_SPLITS)
    xbc = jax.nn.silu(causal_depthwise_conv(xbc, conv_w, conv_b))
    xs, Bm, Cm = split_last(xbc, (SSD_WIDTH, SSD_GROUPS * SSD_STATE, SSD_GROUPS * SSD_STATE))
    dt = jax.nn.softplus((dt_raw + dt_bias).astype(jnp.float32))
    A = -jnp.exp(A_log.astype(jnp.float32))
    xh = xs.reshape(Bsz, T, SSD_HEADS, SSD_HEAD_DIM)
    y = ssd_chunked_scan(xh * dt[..., None], dt * A,
                         Bm.reshape(Bsz, T, SSD_GROUPS, SSD_STATE),
                         Cm.reshape(Bsz, T, SSD_GROUPS, SSD_STATE))
    y = y + xh * D_skip[:, None]
    y = y.reshape(Bsz, T, SSD_WIDTH) * jax.nn.silu(z)
    gs = SSD_WIDTH // SSD_GROUPS
    y = rmsnorm(y.reshape(Bsz, T, SSD_GROUPS, gs), norm_g.reshape(SSD_GROUPS, gs))
    return y.reshape(Bsz, T, SSD_WIDTH).astype(p.dtype) @ w_branch


def peer_ffn(u, w_q, sub_keys, expert_down, expert_up):
    Bsz, T, D = u.shape
    half = PEER_QDIM // 2
    blocks = u.reshape(-1, PEER_TOKEN_BLOCK, D)

    def block_fn(xb):
        q = (xb @ w_q).reshape(PEER_TOKEN_BLOCK, PEER_HEADS, 2, half)
        s = jnp.einsum('thcd,hckd->thck', q, sub_keys).astype(jnp.float32)
        top_s, top_i = lax.top_k(s, PEER_TOPK)
        cand_s = (top_s[:, :, 0, :, None] + top_s[:, :, 1, None, :]).reshape(PEER_TOKEN_BLOCK, PEER_HEADS, -1)
        cand_i = (top_i[:, :, 0, :, None] * PEER_NKEYS + top_i[:, :, 1, None, :]).reshape(PEER_TOKEN_BLOCK, PEER_HEADS, -1)
        best_s, best_pos = lax.top_k(cand_s, PEER_TOPK)
        idx = jnp.take_along_axis(cand_i, best_pos, axis=-1)
        gate = jax.nn.softmax(best_s, axis=-1)
        act = jax.nn.gelu(jnp.einsum('td,thkd->thk', xb, expert_down[idx]).astype(jnp.float32))
        return jnp.einsum('thk,thkd->td', (gate * act).astype(xb.dtype), expert_up[idx])

    return lax.map(block_fn, blocks).reshape(Bsz, T, D)


def setup_inputs(seed: int = 0) -> dict:
    key = jax.random.key(seed)
    ks = iter(jax.random.split(key, 40))
    nrm = lambda shape, scale: jax.random.normal(next(ks), shape, jnp.float32) * scale
    uni = lambda shape, lo, hi: jax.random.uniform(next(ks), shape, jnp.float32, lo, hi)
    L, D = DEPTH, D_MODEL
    dt_init = jnp.exp(uni((L, SSD_HEADS), math.log(1e-3), math.log(1e-1)))
    dt_bias = dt_init + jnp.log(-jnp.expm1(-dt_init))
    return {
        'x': nrm((BATCH, SEQ, D), 1.0),
        'norm_mix_g': 1.0 + nrm((L, D), 0.02),
        'w_in': nrm((L, D, IN_PROJ), D ** -0.5),
        'rwkv_mu': uni((L, RWKV_PROJ), 0.0, 1.0),
        'rwkv_w0': uni((L, RWKV_WIDTH), -5.0, 1.0),
        'rwkv_w2': nrm((L, DECAY_LORA, RWKV_WIDTH), 0.1 * DECAY_LORA ** -0.5),
        'rwkv_a0': nrm((L, RWKV_WIDTH), 0.1),
        'rwkv_a2': nrm((L, ICLR_LORA, RWKV_WIDTH), 0.1 * ICLR_LORA ** -0.5),
        'rwkv_g2': nrm((L, GATE_LORA, RWKV_WIDTH), GATE_LORA ** -0.5),
        'rwkv_v0': 1.0 + nrm((L - 1, RWKV_WIDTH), 0.1),
        'rwkv_v1': nrm((L - 1, RWKV_WIDTH, VRES_LORA), RWKV_WIDTH ** -0.5),
        'rwkv_v2': nrm((L - 1, VRES_LORA, RWKV_WIDTH), VRES_LORA ** -0.5),
        'rwkv_k_k': 0.85 + nrm((L, RWKV_WIDTH), 0.05),
        'rwkv_k_a': 1.0 + nrm((L, RWKV_WIDTH), 0.05),
        'rwkv_r_k': nrm((L, RWKV_HEADS, RWKV_HEAD_SIZE), 0.1),
        'rwkv_ln_w': 1.0 + nrm((L, RWKV_WIDTH), 0.02),
        'rwkv_ln_b': nrm((L, RWKV_WIDTH), 0.02),
        'w_rwkv_branch': nrm((L, RWKV_WIDTH, D), RWKV_WIDTH ** -0.5),
        'ssd_conv_w': nrm((L, SSD_CONV, SSD_CONV_CH), 0.5),
        'ssd_conv_b': nrm((L, SSD_CONV_CH), 0.02),
        'ssd_dt_bias': dt_bias,
        'ssd_A_log': jnp.log(uni((L, SSD_HEADS), 1.0, 16.0)),
        'ssd_D': 1.0 + nrm((L, SSD_HEADS), 0.1),
        'ssd_norm_g': 1.0 + nrm((L, SSD_WIDTH), 0.02),
        'w_ssd_branch': nrm((L, SSD_WIDTH, D), SSD_WIDTH ** -0.5),
        'w_out': nrm((L, D, D), D ** -0.5),
        'norm_ffn_g': 1.0 + nrm((L, D), 0.02),
        'peer_w_q': nrm((L, D, PEER_HEADS * PEER_QDIM), D ** -0.5),
        'peer_sub_keys': nrm((L, PEER_HEADS, 2, PEER_NKEYS, PEER_QDIM // 2), (PEER_QDIM // 2) ** -0.5),
        'peer_down': nrm((L, PEER_EXPERTS, D), D ** -0.5),
        'peer_up': nrm((L, PEER_EXPERTS, D), PEER_HEADS ** -0.5),
        'norm_final_g': 1.0 + nrm((D,), 0.02),
    }


def reference(x, norm_mix_g, w_in, rwkv_mu, rwkv_w0, rwkv_w2, rwkv_a0, rwkv_a2, rwkv_g2,
              rwkv_v0, rwkv_v1, rwkv_v2, rwkv_k_k, rwkv_k_a, rwkv_r_k, rwkv_ln_w, rwkv_ln_b,
              w_rwkv_branch, ssd_conv_w, ssd_conv_b, ssd_dt_bias, ssd_A_log, ssd_D, ssd_norm_g,
              w_ssd_branch, w_out, norm_ffn_g, peer_w_q, peer_sub_keys, peer_down, peer_up,
              norm_final_g):
    h = x
    v_first = None
    for l in range(DEPTH):
        u = rmsnorm(h, norm_mix_g[l])
        proj = u @ w_in[l]
        p_rwkv, p_ssd, gates = split_last(proj, (RWKV_PROJ, SSD_PROJ, GATE_PROJ))
        vres = None if l == 0 else (rwkv_v0[l - 1], rwkv_v1[l - 1], rwkv_v2[l - 1])
        y_a, v_first = rwkv7_time_mix(p_rwkv, v_first, vres, rwkv_mu[l], rwkv_w0[l], rwkv_w2[l],
                                      rwkv_a0[l], rwkv_a2[l], rwkv_g2[l], rwkv_k_k[l], rwkv_k_a[l],
                                      rwkv_r_k[l], rwkv_ln_w[l], rwkv_ln_b[l], w_rwkv_branch[l])
        y_b = ssd_mixer(p_ssd, ssd_conv_w[l], ssd_conv_b[l], ssd_dt_bias[l], ssd_A_log[l], ssd_D[l],
                        ssd_norm_g[l], w_ssd_branch[l])
        g_a, g_b = split_last(gates, (D_MODEL, D_MODEL))
        mixed = jax.nn.sigmoid(g_a) * y_a + jax.nn.sigmoid(g_b) * y_b
        h = h + mixed @ w_out[l]
        h = h + peer_ffn(rmsnorm(h, norm_ffn_g[l]), peer_w_q[l], peer_sub_keys[l], peer_down[l], peer_up[l])
    return rmsnorm(h, norm_final_g)
```

```python
import functools

import jax
import jax.numpy as jnp
from jax import lax
from jax.experimental import pallas as pl
from jax.experimental.pallas import tpu as pltpu

F32 = jnp.float32
BF16 = jnp.bfloat16
HI = lax.Precision.HIGHEST

LANES = 128
SUBLANES = 8
NORM_EPS = 1e-6

D_MODEL = 1024
RW_HEAD = 64
RW_WIDTH = D_MODEL
RW_HEADS = RW_WIDTH // RW_HEAD
RW_PAIRS = RW_WIDTH // LANES
DECAY_LORA, ICLR_LORA, GATE_LORA, VRES_LORA = 64, 64, 160, 32
RW_LORA = DECAY_LORA + ICLR_LORA + GATE_LORA
RW_LORA_PAD = 512
VRES_PAD = 128
RW_GN_EPS = 64e-5
SSD_WIDTH = 2 * D_MODEL
SSD_HEAD = 64
SSD_HEADS = SSD_WIDTH // SSD_HEAD
SSD_STATE = 128
SSD_GROUPS = 4
SSD_GW = SSD_WIDTH // SSD_GROUPS
SSD_HG = SSD_HEADS // SSD_GROUPS
SSD_CONV = 4
SSD_CONV_CH = SSD_WIDTH + 2 * SSD_GROUPS * SSD_STATE
SSD_CHUNK = 128
PEER_HEADS = 8
PEER_NKEYS = 128
PEER_TOPK = 16
PEER_QDIM = 256
PEER_EXPERTS = PEER_NKEYS * PEER_NKEYS
BIG = 1e30

OFF_RKV, W_RKV = 0, 3 * RW_WIDTH
OFF_XBC, W_XBC = 3072, SSD_CONV_CH
OFF_Z, W_Z = 6144, SSD_WIDTH
OFF_GATE, W_GATE = 8192, 2 * D_MODEL
OFF_LORA, W_LORA = 10240, RW_LORA_PAD
OFF_DT, W_DT = 10752, LANES
PROJ_PAD = 10880

VMEM_LIMIT = 56 * 1024 * 1024


def _cparams(sem):
    return pltpu.CompilerParams(dimension_semantics=sem, vmem_limit_bytes=VMEM_LIMIT)


def _full(shape):
    n = len(shape)
    return pl.BlockSpec(shape, lambda *_: (0,) * n)


def _sigmoid(x):
    return 1.0 / (1.0 + jnp.exp(-x))


def _softplus(x):
    return jnp.maximum(x, 0.0) + jnp.log(1.0 + jnp.exp(-jnp.abs(x)))


def _bdot(a, b):
    return jnp.dot(a.astype(BF16), b.astype(BF16), preferred_element_type=F32)


def _xdot(a, b):
    return jnp.dot(a, b, precision=HI, preferred_element_type=F32)


def _norm_matmul_kernel(x_ref, g_ref, w_ref, o_ref, u_ref):
    @pl.when(pl.program_id(1) == 0)
    def _():
        x = x_ref[...]
        ms = jnp.mean(x * x, axis=-1, keepdims=True)
        u_ref[...] = (x * lax.rsqrt(ms + NORM_EPS) * g_ref[...]).astype(BF16)

    o_ref[...] = jnp.dot(u_ref[...], w_ref[...], preferred_element_type=F32)


def _norm_matmul(x, g, w, tm, tn):
    n, d = x.shape
    m = w.shape[1]
    return pl.pallas_call(
        _norm_matmul_kernel,
        grid=(n // tm, m // tn),
        in_specs=[
            pl.BlockSpec((tm, d), lambda i, j: (i, 0)),
            pl.BlockSpec((1, d), lambda i, j: (0, 0)),
            pl.BlockSpec((d, tn), lambda i, j: (0, j)),
        ],
        out_specs=pl.BlockSpec((tm, tn), lambda i, j: (i, j)),
        out_shape=jax.ShapeDtypeStruct((n, m), F32),
        scratch_shapes=[pltpu.VMEM((tm, d), BF16)],
        compiler_params=_cparams(("parallel", "arbitrary")),
        name="in_proj",
    )(x, g, w)


def _shift_mix(x, prev_ref, mu, first):
    rows = lax.broadcasted_iota(jnp.int32, x.shape, 0)
    prev = jnp.where(first, 0.0, prev_ref[...])
    xs = jnp.where(rows == 0, prev, pltpu.roll(x, 1, 0))
    prev_ref[...] = x[x.shape[0] - 1:, :]
    return x + (xs - x) * mu


def _rwkv_prep_kernel(has_vres, *refs):
    if has_vres:
        (prkv_ref, plora_ref, vf_ref, mu1_ref, mu2_ref, w0_ref, w2_ref, a0_ref, a2_ref, g2_ref,
         kk_ref, ka_ref, rk_ref, e_ref, et_ref, v0_ref, v1_ref, v2_ref,
         r_o, w_o, k_o, a_o, b_o, g_o, bonus_o, vt_o, prev1, prev2) = refs
    else:
        (prkv_ref, plora_ref, mu1_ref, mu2_ref, w0_ref, w2_ref, a0_ref, a2_ref, g2_ref,
         kk_ref, ka_ref, rk_ref, e_ref, et_ref,
         r_o, w_o, k_o, a_o, b_o, g_o, bonus_o, vt_o, v_o, prev1, prev2) = refs
    first = pl.program_id(1) == 0
    p = _shift_mix(prkv_ref[...], prev1, mu1_ref[...], first)
    lo = _shift_mix(plora_ref[...], prev2, mu2_ref[...], first)
    r = p[:, :RW_WIDTH]
    k = p[:, RW_WIDTH:2 * RW_WIDTH]
    v = p[:, 2 * RW_WIDTH:]
    w_log = -_softplus(-(w0_ref[...] + _bdot(jnp.tanh(lo), w2_ref[...]))) - 0.5
    decay = jnp.exp(-jnp.exp(w_log))
    a = _sigmoid(a0_ref[...] + _bdot(lo, a2_ref[...]))
    g = _bdot(_sigmoid(lo), g2_ref[...])
    if has_vres:
        mix = _sigmoid(v0_ref[...] + _bdot(_bdot(v, v1_ref[...]), v2_ref[...]))
        v = v + (vf_ref[...] - v) * mix
    else:
        v_o[...] = v
    e, et = e_ref[...], et_ref[...]
    kk = k * kk_ref[...]
    inv = lax.rsqrt(_xdot(kk * kk, e) + 1e-12)
    kk = kk * _xdot(inv, et)
    k2 = k * (1.0 + (a - 1.0) * ka_ref[...])
    rk = _xdot(_xdot(r * k2 * rk_ref[...], e), et)
    r_o[...] = r
    w_o[...] = decay
    k_o[...] = k2
    a_o[...] = -kk
    b_o[...] = kk * a
    g_o[...] = g
    bonus_o[...] = rk * v
    vt_o[...] = v.T


def _rwkv_prep(proj3, vfirst, prm, tt):
    b, t, _ = proj3.shape
    has_vres = vfirst is not None
    tok = lambda w, c: pl.BlockSpec((None, tt, w), lambda bi, i: (bi, i, c))
    in_specs = [tok(W_RKV, OFF_RKV // W_RKV), tok(W_LORA, OFF_LORA // W_LORA)]
    args = [proj3, proj3]
    if has_vres:
        in_specs.append(tok(RW_WIDTH, 0))
        args.append(vfirst)
    names = ["mu1", "mu2", "w0", "w2", "a0", "a2", "g2", "k_k", "k_a", "r_k", "seg", "seg_t"]
    if has_vres:
        names += ["v0", "v1", "v2"]
    for nm in names:
        in_specs.append(_full(prm[nm].shape))
        args.append(prm[nm])
    n_tok_out = 7 if has_vres else 7
    out_shape = [jax.ShapeDtypeStruct((b, t, RW_WIDTH), F32)] * n_tok_out
    out_specs = [tok(RW_WIDTH, 0)] * n_tok_out
    out_shape.append(jax.ShapeDtypeStruct((b, RW_WIDTH, t), F32))
    out_specs.append(pl.BlockSpec((None, RW_WIDTH, tt), lambda bi, i: (bi, 0, i)))
    if not has_vres:
        out_shape.append(jax.ShapeDtypeStruct((b, t, RW_WIDTH), F32))
        out_specs.append(tok(RW_WIDTH, 0))
    return pl.pallas_call(
        functools.partial(_rwkv_prep_kernel, has_vres),
        grid=(b, t // tt),
        in_specs=in_specs,
        out_specs=out_specs,
        out_shape=out_shape,
        scratch_shapes=[pltpu.VMEM((1, W_RKV), F32), pltpu.VMEM((1, W_LORA), F32)],
        compiler_params=_cparams(("parallel", "arbitrary")),
        name="rwkv_prep",
    )(*args)


def _rwkv_rec_kernel(r_ref, w_ref, k_ref, a_ref, b_ref, vt_ref, yt_ref, s_ref, y_acc):
    tc = r_ref.shape[0]

    @pl.when(pl.program_id(1) == 0)
    def _():
        s_ref[...] = jnp.zeros_like(s_ref)

    y_acc[...] = jnp.zeros_like(y_acc)
    is_lo = lax.broadcasted_iota(jnp.int32, (SUBLANES, LANES), 1) < RW_HEAD
    lane_t = lax.broadcasted_iota(jnp.int32, (RW_HEAD, tc), 1)

    def halves(ref, rows, sl):
        x = ref[rows, sl]
        return jnp.where(is_lo, x, 0.0), jnp.where(is_lo, 0.0, x)

    def group(j8, carry):
        base = pl.multiple_of(j8 * SUBLANES, SUBLANES)
        rows = pl.ds(base, SUBLANES)
        for p in range(RW_PAIRS):
            sl = slice(LANES * p, LANES * (p + 1))
            ev = slice(LANES * p, LANES * p + RW_HEAD)
            od = slice(LANES * p + RW_HEAD, LANES * (p + 1))
            a_lo, a_hi = halves(a_ref, rows, sl)
            b_lo, b_hi = halves(b_ref, rows, sl)
            k_lo, k_hi = halves(k_ref, rows, sl)
            r_lo, r_hi = halves(r_ref, rows, sl)
            w = w_ref[rows, sl]
            vt_e, vt_o = vt_ref[ev, :], vt_ref[od, :]
            s = s_ref[p]
            yg_e = jnp.zeros((RW_HEAD, tc), F32)
            yg_o = jnp.zeros((RW_HEAD, tc), F32)
            for i in range(SUBLANES):
                ri = slice(i, i + 1)
                sel = lane_t == base + i
                sa_e = jnp.sum(s * a_lo[ri], axis=-1, keepdims=True)
                sa_o = jnp.sum(s * a_hi[ri], axis=-1, keepdims=True)
                v_e = jnp.sum(jnp.where(sel, vt_e, 0.0), axis=-1, keepdims=True)
                v_o = jnp.sum(jnp.where(sel, vt_o, 0.0), axis=-1, keepdims=True)
                s = s * w[ri] + sa_e * b_lo[ri] + sa_o * b_hi[ri] + v_e * k_lo[ri] + v_o * k_hi[ri]
                y_e = jnp.sum(s * r_lo[ri], axis=-1, keepdims=True)
                y_o = jnp.sum(s * r_hi[ri], axis=-1, keepdims=True)
                yg_e = jnp.where(sel, y_e, yg_e)
                yg_o = jnp.where(sel, y_o, yg_o)
            s_ref[p] = s
            y_acc[ev, :] += yg_e
            y_acc[od, :] += yg_o
        return carry

    lax.fori_loop(0, tc // SUBLANES, group, 0)
    yt_ref[...] = y_acc[...]


def _rwkv_rec(r, w, k, a, bb, vt, tc):
    b, t, c = r.shape
    tok = pl.BlockSpec((None, tc, c), lambda bi, i: (bi, i, 0))
    tr = pl.BlockSpec((None, c, tc), lambda bi, i: (bi, 0, i))
    return pl.pallas_call(
        _rwkv_rec_kernel,
        grid=(b, t // tc),
        in_specs=[tok, tok, tok, tok, tok, tr],
        out_specs=tr,
        out_shape=jax.ShapeDtypeStruct((b, c, t), F32),
        scratch_shapes=[pltpu.VMEM((RW_PAIRS, RW_HEAD, LANES), F32), pltpu.VMEM((c, tc), F32)],
        compiler_params=_cparams(("parallel", "arbitrary")),
        name="rwkv_rec",
    )(r, w, k, a, bb, vt)


def _rwkv_post_kernel(yt_ref, bonus_ref, g_ref, lnw_ref, lnb_ref, e_ref, et_ref, wb_ref, o_ref):
    y = yt_ref[...].T
    e, et = e_ref[...], et_ref[...]
    mean = _xdot(_xdot(y, e), et) * (1.0 / RW_HEAD)
    yc = y - mean
    var = _xdot(_xdot(yc * yc, e), et) * (1.0 / RW_HEAD)
    y = yc * lax.rsqrt(var + RW_GN_EPS) * lnw_ref[...] + lnb_ref[...]
    y = (y + bonus_ref[...]) * g_ref[...]
    o_ref[...] = _bdot(y, wb_ref[...])


def _rwkv_post(yt, bonus, g, prm, tt):
    b, c, t = yt.shape
    tok = pl.BlockSpec((None, tt, c), lambda bi, i: (bi, i, 0))
    names = ["ln_w", "ln_b", "seg", "seg_t", "w_branch"]
    return pl.pallas_call(
        _rwkv_post_kernel,
        grid=(b, t // tt),
        in_specs=[pl.BlockSpec((None, c, tt), lambda bi, i: (bi, 0, i)), tok, tok] + [_full(prm[nm].shape) for nm in names],
        out_specs=pl.BlockSpec((None, tt, D_MODEL), lambda bi, i: (bi, i, 0)),
        out_shape=jax.ShapeDtypeStruct((b, t, D_MODEL), F32),
        compiler_params=_cparams(("parallel", "parallel")),
        name="rwkv_post",
    )(yt, bonus, g, *[prm[nm] for nm in names])


def _ssd_kernel(z_ref, xbc_ref, dt_ref, cw_ref, cb_ref, dtb_ref, alog_ref, dskip_ref, ng_ref, ex_ref, wb_ref,
                o_ref, buf, st_ref):
    L = SSD_CHUNK
    first = pl.program_id(1) == 0

    @pl.when(first)
    def _():
        buf[0:8, :] = jnp.zeros((8, SSD_CONV_CH), F32)
        st_ref[...] = jnp.zeros_like(st_ref)

    buf[8:8 + L, :] = xbc_ref[...]
    conv = cb_ref[...]
    for j in range(SSD_CONV):
        off = 8 - (SSD_CONV - 1) + j
        conv = conv + buf[off:off + L, :] * cw_ref[j:j + 1, :]
    buf[0:8, :] = buf[L:L + 8, :]
    xa = conv * _sigmoid(conv)
    xs = xa[:, :SSD_WIDTH]

    dt = _softplus(dt_ref[...] + dtb_ref[...])
    a = dt * (-jnp.exp(alog_ref[...]))
    row = lax.broadcasted_iota(jnp.int32, (L, L), 0)
    col = lax.broadcasted_iota(jnp.int32, (L, L), 1)
    causal = row >= col
    acs = _xdot(causal.astype(F32), a)
    acs_last = acs[L - 1:L, :]
    stacked = jnp.concatenate(
        [dt, jnp.exp(acs), jnp.exp(acs_last - acs), jnp.broadcast_to(jnp.exp(acs_last), (8, LANES))], axis=0)
    wide = _xdot(stacked, ex_ref[...])
    dt_x, ea_x, dte_x, el_x = wide[:L], wide[L:2 * L], wide[2 * L:3 * L], wide[3 * L:3 * L + 1]
    acs_t = acs.T
    xdt = xs * dt_x
    lane = lax.broadcasted_iota(jnp.int32, (L, LANES), 1)
    is_lo = lane < SSD_HEAD

    y_parts = []
    for g in range(SSD_GROUPS):
        bg = xa[:, SSD_WIDTH + SSD_STATE * g:SSD_WIDTH + SSD_STATE * (g + 1)]
        cg = xa[:, SSD_WIDTH + SSD_STATE * (SSD_GROUPS + g):SSD_WIDTH + SSD_STATE * (SSD_GROUPS + g + 1)]
        cb = lax.dot_general(cg.astype(BF16), bg.astype(BF16), (((1,), (1,)), ((), ())), preferred_element_type=F32)
        gl = slice(SSD_GW * g, SSD_GW * (g + 1))
        y_off = _bdot(cg, st_ref[g]) * ea_x[:, gl]
        diag = []
        for pp in range(SSD_HG // 2):
            acc = None
            xp = xdt[:, SSD_GW * g + LANES * pp:SSD_GW * g + LANES * (pp + 1)]
            for half in range(2):
                h = SSD_HG * g + 2 * pp + half
                seg = acs[:, h:h + 1] - acs_t[h:h + 1, :]
                lm = jnp.exp(jnp.where(causal, seg, -BIG))
                xh = jnp.where(is_lo if half == 0 else jnp.logical_not(is_lo), xp, 0.0)
                term = _bdot(lm * cb, xh)
                acc = term if acc is None else acc + term
            diag.append(acc)
        y_parts.append(jnp.concatenate(diag, axis=1) + y_off)
        st_ref[g] = st_ref[g] * el_x[:, gl] + _bdot(bg.T, xdt[:, gl] * dte_x[:, gl])
    y = jnp.concatenate(y_parts, axis=1)
    y = y + xs * dskip_ref[...]
    z = z_ref[...]
    y = y * (z * _sigmoid(z))
    normed = []
    for g in range(SSD_GROUPS):
        yg = y[:, SSD_GW * g:SSD_GW * (g + 1)]
        ms = jnp.mean(yg * yg, axis=-1, keepdims=True)
        normed.append(yg * lax.rsqrt(ms + NORM_EPS))
    y = jnp.concatenate(normed, axis=1) * ng_ref[...]
    o_ref[...] = _bdot(y, wb_ref[...])


def _ssd(proj3, prm):
    b, t, _ = proj3.shape
    L = SSD_CHUNK
    tok = lambda w, off: pl.BlockSpec((None, L, w), lambda bi, i: (bi, i, off // w))
    names = ["conv_w", "conv_b", "dt_bias", "a_log", "d_skip", "norm_g", "expand", "w_branch"]
    return pl.pallas_call(
        _ssd_kernel,
        grid=(b, t // L),
        in_specs=[tok(W_Z, OFF_Z), tok(W_XBC, OFF_XBC), tok(W_DT, OFF_DT)] + [_full(prm[nm].shape) for nm in names],
        out_specs=pl.BlockSpec((None, L, D_MODEL), lambda bi, i: (bi, i, 0)),
        out_shape=jax.ShapeDtypeStruct((b, t, D_MODEL), F32),
        scratch_shapes=[pltpu.VMEM((L + 8, SSD_CONV_CH), F32), pltpu.VMEM((SSD_GROUPS, SSD_STATE, SSD_GW), F32)],
        compiler_params=_cparams(("parallel", "arbitrary")),
        name="ssd",
    )(proj3, proj3, proj3, *[prm[nm] for nm in names])


def _mix_kernel(ya_ref, yb_ref, gate_ref, h_ref, wo_ref, gf_ref, wq_ref, keys_ref, h_o, ut_o, s_o):
    gt = gate_ref[...]
    mixed = _sigmoid(gt[:, :D_MODEL]) * ya_ref[...] + _sigmoid(gt[:, D_MODEL:]) * yb_ref[...]
    h = h_ref[...] + _bdot(mixed, wo_ref[...])
    h_o[...] = h
    ms = jnp.mean(h * h, axis=-1, keepdims=True)
    u = h * lax.rsqrt(ms + NORM_EPS) * gf_ref[...]
    ut_o[...] = u.T.astype(BF16)
    q = _bdot(u, wq_ref[...])
    for j in range(2 * PEER_HEADS):
        sl = slice(LANES * j, LANES * (j + 1))
        s_o[:, sl] = _bdot(q[:, sl], keys_ref[j])


def _mix(ya, yb, proj, h, prm, tm):
    n = h.shape[0]
    names = ["w_out", "norm_ffn_g", "w_q", "keys_t"]
    tok = lambda w, c: pl.BlockSpec((tm, w), lambda i: (i, c))
    return pl.pallas_call(
        _mix_kernel,
        grid=(n // tm,),
        in_specs=[tok(D_MODEL, 0), tok(D_MODEL, 0), tok(W_GATE, OFF_GATE // W_GATE), tok(D_MODEL, 0)]
        + [_full(prm[nm].shape) for nm in names],
        out_specs=[tok(D_MODEL, 0), pl.BlockSpec((D_MODEL, tm), lambda i: (0, i)), tok(2 * PEER_HEADS * PEER_NKEYS, 0)],
        out_shape=[jax.ShapeDtypeStruct((n, D_MODEL), F32), jax.ShapeDtypeStruct((D_MODEL, n), BF16),
                   jax.ShapeDtypeStruct((n, 2 * PEER_HEADS * PEER_NKEYS), F32)],
        compiler_params=_cparams(("parallel",)),
        name="mix",
    )(ya, yb, proj, h, *[prm[nm] for nm in names])


def _top_values(x, n, lane):
    tops = jnp.zeros((x.shape[0], LANES), F32)
    m = None
    for k in range(n):
        m = jnp.max(x, axis=-1, keepdims=True)
        tops = jnp.where(lane == k, m, tops)
        if k + 1 < n:
            x = jnp.where(x >= m, -BIG, x)
    return tops, m


def _peer_sel_kernel(s_ref, x1_ref, x2_ref, e1_o, t1_o, e2_o, d2_o):
    tm = s_ref.shape[0]
    lane = lax.broadcasted_iota(jnp.int32, (tm, LANES), 1)
    for h in range(PEER_HEADS):
        s1 = s_ref[:, 2 * LANES * h:2 * LANES * h + LANES]
        s2 = s_ref[:, 2 * LANES * h + LANES:2 * LANES * (h + 1)]
        top1, thr1 = _top_values(s1, PEER_TOPK, lane)
        top2, thr2 = _top_values(s2, PEER_TOPK, lane)
        cand = _xdot(top1, x1_ref[...]) + _xdot(top2, x2_ref[...])
        m0 = None
        zsum = None
        c_prev = None
        x = cand
        for k in range(PEER_TOPK + 1):
            m = jnp.max(x, axis=-1, keepdims=True)
            if k == 0:
                m0 = m
                zsum = jnp.ones_like(m)
            elif k < PEER_TOPK:
                zsum = zsum + jnp.exp(m - m0)
            if k == PEER_TOPK - 1:
                c_prev = m
            if k < PEER_TOPK:
                x = jnp.where(x >= m, -BIG, x)
        thr = 0.5 * (c_prev + m)
        in1 = s1 >= thr1
        in2 = s2 >= thr2
        m1 = jnp.max(s1, axis=-1, keepdims=True)
        m2 = jnp.max(s2, axis=-1, keepdims=True)
        e1 = jnp.where(in1, jnp.exp(s1 - m1), 0.0) / zsum
        e2 = jnp.where(in2, jnp.exp(s2 - m2), 0.0)
        e1_o[h] = e1.T
        t1_o[h] = jnp.where(in1, thr - s1, BIG).T
        e2_o[h] = e2.T
        d2_o[h] = jnp.where(in2, s2, -BIG).T


def _peer_sel(s, prm, tm):
    n = s.shape[0]
    out = jax.ShapeDtypeStruct((PEER_HEADS, PEER_NKEYS, n), F32)
    ospec = pl.BlockSpec((PEER_HEADS, PEER_NKEYS, tm), lambda i: (0, 0, i))
    return pl.pallas_call(
        _peer_sel_kernel,
        grid=(n // tm,),
        in_specs=[pl.BlockSpec((tm, s.shape[1]), lambda i: (i, 0)), _full(prm["cand1"].shape), _full(prm["cand2"].shape)],
        out_specs=[ospec] * 4,
        out_shape=[out] * 4,
        compiler_params=_cparams(("parallel",)),
        name="peer_sel",
    )(s, prm["cand1"], prm["cand2"])


def _gelu(x):
    return 0.5 * x * (1.0 + jnp.tanh(0.7978845608028654 * (x + 0.044715 * x * x * x)))


def _peer_ffn_kernel(final_g, ut_ref, down_ref, upt_ref, e1_ref, t1_ref, e2_ref, d2_ref, h_ref, *rest):
    if final_g:
        gfin_ref, o_ref, acc, act, hs = rest
    else:
        o_ref, acc, act, hs = rest
    j = pl.program_id(1)
    te, tl = act.shape
    ni = te // PEER_NKEYS

    @pl.when(j == 0)
    def _():
        acc[...] = jnp.zeros_like(acc)

    act[...] = jnp.dot(down_ref[...], ut_ref[...], preferred_element_type=F32)

    def first_key(ii, carry):
        rows = pl.ds(pl.multiple_of(ii * PEER_NKEYS, PEER_NKEYS), PEER_NKEYS)
        for c in range(tl // LANES):
            cl = slice(LANES * c, LANES * (c + 1))
            gmat = jnp.zeros((PEER_NKEYS, LANES), F32)
            for h in range(PEER_HEADS):
                t1 = t1_ref[ii, h:h + 1, cl]
                e1 = e1_ref[ii, h:h + 1, cl]
                gmat = gmat + jnp.where(d2_ref[h, :, cl] >= t1, e2_ref[h, :, cl], 0.0) * e1
            hs[rows, cl] = (gmat * _gelu(act[rows, cl])).astype(BF16)
        return carry

    lax.fori_loop(0, ni, first_key, 0)
    acc[...] += jnp.dot(upt_ref[...], hs[...], preferred_element_type=F32)

    @pl.when(j == pl.num_programs(1) - 1)
    def _():
        h = h_ref[...] + acc[...].T
        if final_g:
            ms = jnp.mean(h * h, axis=-1, keepdims=True)
            h = h * lax.rsqrt(ms + NORM_EPS) * gfin_ref[...]
        o_ref[...] = h


def _peer_ffn(ut, down, upt, e1, t1, e2, d2, h, final_g, tl, te):
    n = h.shape[0]
    sel = pl.BlockSpec((PEER_HEADS, PEER_NKEYS, tl), lambda i, j: (0, 0, i))
    sel1 = pl.BlockSpec((te // PEER_NKEYS, PEER_HEADS, tl), lambda i, j: (j, 0, i))
    in_specs = [
        pl.BlockSpec((D_MODEL, tl), lambda i, j: (0, i)),
        pl.BlockSpec((te, D_MODEL), lambda i, j: (j, 0)),
        pl.BlockSpec((D_MODEL, te), lambda i, j: (0, j)),
        sel1, sel1, sel, sel,
        pl.BlockSpec((tl, D_MODEL), lambda i, j: (i, 0)),
    ]
    args = [ut, down, upt, e1, t1, e2, d2, h]
    if final_g is not None:
        in_specs.append(_full(final_g.shape))
        args.append(final_g)
    return pl.pallas_call(
        functools.partial(_peer_ffn_kernel, final_g is not None),
        grid=(n // tl, PEER_EXPERTS // te),
        in_specs=in_specs,
        out_specs=pl.BlockSpec((tl, D_MODEL), lambda i, j: (i, 0)),
        out_shape=jax.ShapeDtypeStruct((n, D_MODEL), F32),
        scratch_shapes=[pltpu.VMEM((D_MODEL, tl), F32), pltpu.VMEM((te, tl), F32), pltpu.VMEM((te, tl), BF16)],
        compiler_params=_cparams(("parallel", "arbitrary")),
        name="peer_ffn",
    )(*args)


def _pad_rows(w, start, total):
    return jnp.zeros((total, w.shape[1]), w.dtype).at[start:start + w.shape[0]].set(w)


def _pad_lanes(v, total):
    return jnp.zeros((1, total), v.dtype).at[0, :v.shape[0]].set(v)


def _row(v):
    return v.reshape(1, -1)


def _constants():
    c = jnp.arange(RW_WIDTH)[:, None] // RW_HEAD == jnp.arange(LANES)[None, :]
    seg = c.astype(F32)
    expand = (jnp.arange(LANES)[:, None] == jnp.arange(SSD_WIDTH)[None, :] // SSD_HEAD).astype(F32)
    cc = jnp.arange(PEER_TOPK * PEER_TOPK)[None, :]
    kk = jnp.arange(LANES)[:, None]
    cand1 = (cc // PEER_TOPK == kk).astype(F32)
    cand2 = (cc % PEER_TOPK == kk).astype(F32)
    return seg, seg.T, expand, cand1, cand2


def kernel(x, norm_mix_g, w_in, rwkv_mu, rwkv_w0, rwkv_w2, rwkv_a0, rwkv_a2, rwkv_g2, rwkv_v0, rwkv_v1, rwkv_v2,
           rwkv_k_k, rwkv_k_a, rwkv_r_k, rwkv_ln_w, rwkv_ln_b, w_rwkv_branch, ssd_conv_w, ssd_conv_b, ssd_dt_bias,
           ssd_A_log, ssd_D, ssd_norm_g, w_ssd_branch, w_out, norm_ffn_g, peer_w_q, peer_sub_keys, peer_down, peer_up,
           norm_final_g):
    bsz, t, d = x.shape
    n = bsz * t
    depth = w_in.shape[0]
    seg, seg_t, expand, cand1, cand2 = _constants()
    h = x.reshape(n, d)
    v_first = None
    o_lora = 3 * RW_WIDTH
    o_z = o_lora + RW_LORA
    o_xbc = o_z + SSD_WIDTH
    o_dt = o_xbc + SSD_CONV_CH
    o_gate = o_dt + SSD_HEADS
    tt = min(256, t)
    for l in range(depth):
        wl = w_in[l]
        zpad = lambda k: jnp.zeros((d, k), wl.dtype)
        w_cat = jnp.concatenate([
            wl[:, :o_lora], wl[:, o_xbc:o_dt], wl[:, o_z:o_xbc], wl[:, o_gate:],
            wl[:, o_lora:o_z], zpad(RW_LORA_PAD - RW_LORA), wl[:, o_dt:o_gate], zpad(LANES - SSD_HEADS)], axis=1).astype(BF16)
        proj = _norm_matmul(h, _row(norm_mix_g[l]), w_cat, tm=min(512, n), tn=640)
        proj3 = proj.reshape(bsz, t, PROJ_PAD)

        mu = rwkv_mu[l]
        rw = {
            "mu1": _row(mu[:o_lora]), "mu2": _pad_lanes(mu[o_lora:], RW_LORA_PAD),
            "w0": _row(rwkv_w0[l]), "w2": _pad_rows(rwkv_w2[l], 0, RW_LORA_PAD).astype(BF16),
            "a0": _row(rwkv_a0[l]), "a2": _pad_rows(rwkv_a2[l], DECAY_LORA, RW_LORA_PAD).astype(BF16),
            "g2": _pad_rows(rwkv_g2[l], DECAY_LORA + ICLR_LORA, RW_LORA_PAD).astype(BF16),
            "k_k": _row(rwkv_k_k[l]), "k_a": _row(rwkv_k_a[l]), "r_k": _row(rwkv_r_k[l]),
            "seg": seg, "seg_t": seg_t,
            "ln_w": _row(rwkv_ln_w[l]), "ln_b": _row(rwkv_ln_b[l]), "w_branch": w_rwkv_branch[l].astype(BF16),
        }
        if l > 0:
            rw["v0"] = _row(rwkv_v0[l - 1])
            rw["v1"] = jnp.zeros((RW_WIDTH, VRES_PAD), F32).at[:, :VRES_LORA].set(rwkv_v1[l - 1]).astype(BF16)
            rw["v2"] = _pad_rows(rwkv_v2[l - 1], 0, VRES_PAD).astype(BF16)
        outs = _rwkv_prep(proj3, v_first, rw, tt)
        r_, w_, k_, a_, b_, g_, bonus, vt = outs[:8]
        if l == 0:
            v_first = outs[8]
        yt = _rwkv_rec(r_, w_, k_, a_, b_, vt, tc=min(LANES, t))
        y_a = _rwkv_post(yt, bonus, g_, rw, tt)

        sp = {
            "conv_w": ssd_conv_w[l], "conv_b": _row(ssd_conv_b[l]),
            "dt_bias": _pad_lanes(ssd_dt_bias[l], LANES), "a_log": _pad_lanes(ssd_A_log[l], LANES),
            "d_skip": _row(jnp.repeat(ssd_D[l], SSD_HEAD)), "norm_g": _row(ssd_norm_g[l]),
            "expand": expand, "w_branch": w_ssd_branch[l].astype(BF16),
        }
        y_b = _ssd(proj3, sp)

        pp = {
            "w_out": w_out[l].astype(BF16), "norm_ffn_g": _row(norm_ffn_g[l]), "w_q": peer_w_q[l].astype(BF16),
            "keys_t": jnp.swapaxes(peer_sub_keys[l].reshape(2 * PEER_HEADS, PEER_NKEYS, PEER_QDIM // 2), 1, 2).astype(BF16),
            "cand1": cand1, "cand2": cand2,
        }
        h, ut, s = _mix(y_a.reshape(n, d), y_b.reshape(n, d), proj, h, pp, tm=min(256, n))
        e1, t1, e2, d2 = _peer_sel(s, pp, tm=min(256, n))
        fin = _row(norm_final_g) if l == depth - 1 else None
        e1, t1 = jnp.swapaxes(e1, 0, 1), jnp.swapaxes(t1, 0, 1)
        h = _peer_ffn(ut, peer_down[l].astype(BF16), peer_up[l].T.astype(BF16), e1, t1, e2, d2, h, fin,
                      tl=min(512, n), te=2 * PEER_NKEYS)
    return h.reshape(bsz, t, d)
```

```python
import functools

import jax
import jax.numpy as jnp
from jax import lax
from jax.experimental import pallas as pl
from jax.experimental.pallas import tpu as pltpu

F32 = jnp.float32
BF16 = jnp.bfloat16
HI = lax.Precision.HIGHEST

LANES = 128
SUBLANES = 8
NORM_EPS = 1e-6

D_MODEL = 1024
RW_HEAD = 64
RW_WIDTH = D_MODEL
RW_HEADS = RW_WIDTH // RW_HEAD
RW_PAIRS = RW_WIDTH // LANES
DECAY_LORA, ICLR_LORA, GATE_LORA, VRES_LORA = 64, 64, 160, 32
RW_LORA = DECAY_LORA + ICLR_LORA + GATE_LORA
RW_LORA_PAD = 512
VRES_PAD = 128
RW_GN_EPS = 64e-5
SSD_WIDTH = 2 * D_MODEL
SSD_HEAD = 64
SSD_HEADS = SSD_WIDTH // SSD_HEAD
SSD_STATE = 128
SSD_GROUPS = 4
SSD_GW = SSD_WIDTH // SSD_GROUPS
SSD_HG = SSD_HEADS // SSD_GROUPS
SSD_CONV = 4
SSD_CONV_CH = SSD_WIDTH + 2 * SSD_GROUPS * SSD_STATE
SSD_CHUNK = 128
PEER_HEADS = 8
PEER_NKEYS = 128
PEER_TOPK = 16
PEER_QDIM = 256
PEER_EXPERTS = PEER_NKEYS * PEER_NKEYS
BIG = 1e30

OFF_RKV, W_RKV = 0, 3 * RW_WIDTH
OFF_XBC, W_XBC = 3072, SSD_CONV_CH
OFF_Z, W_Z = 6144, SSD_WIDTH
OFF_GATE, W_GATE = 8192, 2 * D_MODEL
OFF_LORA, W_LORA = 10240, RW_LORA_PAD
OFF_DT, W_DT = 10752, LANES
PROJ_PAD = 10880

VMEM_LIMIT = 56 * 1024 * 1024


def _cparams(sem):
    return pltpu.CompilerParams(dimension_semantics=sem, vmem_limit_bytes=VMEM_LIMIT)


def _full(shape):
    n = len(shape)
    return pl.BlockSpec(shape, lambda *_: (0,) * n)


def _sigmoid(x):
    return 1.0 / (1.0 + jnp.exp(-x))


def _softplus(x):
    return jnp.maximum(x, 0.0) + jnp.log(1.0 + jnp.exp(-jnp.abs(x)))


def _bdot(a, b):
    return jnp.dot(a.astype(BF16), b.astype(BF16), preferred_element_type=F32)


def _xdot(a, b):
    return jnp.dot(a, b, precision=HI, preferred_element_type=F32)


def _norm_matmul_kernel(x_ref, g_ref, w_ref, o_ref, u_ref):
    @pl.when(pl.program_id(1) == 0)
    def _():
        x = x_ref[...]
        ms = jnp.mean(x * x, axis=-1, keepdims=True)
        u_ref[...] = (x * lax.rsqrt(ms + NORM_EPS) * g_ref[...]).astype(BF16)

    o_ref[...] = jnp.dot(u_ref[...], w_ref[...], preferred_element_type=F32)


def _norm_matmul(x, g, w, tm, tn):
    n, d = x.shape
    m = w.shape[1]
    return pl.pallas_call(
        _norm_matmul_kernel,
        grid=(n // tm, m // tn),
        in_specs=[
            pl.BlockSpec((tm, d), lambda i, j: (i, 0)),
            pl.BlockSpec((1, d), lambda i, j: (0, 0)),
            pl.BlockSpec((d, tn), lambda i, j: (0, j)),
        ],
        out_specs=pl.BlockSpec((tm, tn), lambda i, j: (i, j)),
        out_shape=jax.ShapeDtypeStruct((n, m), F32),
        scratch_shapes=[pltpu.VMEM((tm, d), BF16)],
        compiler_params=_cparams(("parallel", "arbitrary")),
        name="in_proj",
    )(x, g, w)


def _shift_mix(x, prev_ref, mu, first):
    rows = lax.broadcasted_iota(jnp.int32, x.shape, 0)
    prev = jnp.where(first, 0.0, prev_ref[...])
    xs = jnp.where(rows == 0, prev, pltpu.roll(x, 1, 0))
    prev_ref[...] = x[x.shape[0] - 1:, :]
    return x + (xs - x) * mu


def _rwkv_prep_kernel(has_vres, *refs):
    if has_vres:
        (prkv_ref, plora_ref, vf_ref, mu1_ref, mu2_ref, w0_ref, w2_ref, a0_ref, a2_ref, g2_ref,
         kk_ref, ka_ref, rk_ref, e_ref, et_ref, v0_ref, v1_ref, v2_ref,
         r_o, w_o, k_o, a_o, b_o, g_o, bonus_o, vt_o, prev1, prev2) = refs
    else:
        (prkv_ref, plora_ref, mu1_ref, mu2_ref, w0_ref, w2_ref, a0_ref, a2_ref, g2_ref,
         kk_ref, ka_ref, rk_ref, e_ref, et_ref,
         r_o, w_o, k_o, a_o, b_o, g_o, bonus_o, vt_o, v_o, prev1, prev2) = refs
    first = pl.program_id(1) == 0
    p = _shift_mix(prkv_ref[...], prev1, mu1_ref[...], first)
    lo = _shift_mix(plora_ref[...], prev2, mu2_ref[...], first)
    r = p[:, :RW_WIDTH]
    k = p[:, RW_WIDTH:2 * RW_WIDTH]
    v = p[:, 2 * RW_WIDTH:]
    w_log = -_softplus(-(w0_ref[...] + _bdot(jnp.tanh(lo), w2_ref[...]))) - 0.5
    decay = jnp.exp(-jnp.exp(w_log))
    a = _sigmoid(a0_ref[...] + _bdot(lo, a2_ref[...]))
    g = _bdot(_sigmoid(lo), g2_ref[...])
    if has_vres:
        mix = _sigmoid(v0_ref[...] + _bdot(_bdot(v, v1_ref[...]), v2_ref[...]))
        v = v + (vf_ref[...] - v) * mix
    else:
        v_o[...] = v
    e, et = e_ref[...], et_ref[...]
    kk = k * kk_ref[...]
    inv = lax.rsqrt(_xdot(kk * kk, e) + 1e-12)
    kk = kk * _xdot(inv, et)
    k2 = k * (1.0 + (a - 1.0) * ka_ref[...])
    rk = _xdot(_xdot(r * k2 * rk_ref[...], e), et)
    r_o[...] = r
    w_o[...] = decay
    k_o[...] = k2
    a_o[...] = -kk
    b_o[...] = kk * a
    g_o[...] = g
    bonus_o[...] = rk * v
    vt_o[...] = v.T


def _rwkv_prep(proj3, vfirst, prm, tt):
    b, t, _ = proj3.shape
    has_vres = vfirst is not None
    tok = lambda w, c: pl.BlockSpec((None, tt, w), lambda bi, i: (bi, i, c))
    in_specs = [tok(W_RKV, OFF_RKV // W_RKV), tok(W_LORA, OFF_LORA // W_LORA)]
    args = [proj3, proj3]
    if has_vres:
        in_specs.append(tok(RW_WIDTH, 0))
        args.append(vfirst)
    names = ["mu1", "mu2", "w0", "w2", "a0", "a2", "g2", "k_k", "k_a", "r_k", "seg", "seg_t"]
    if has_vres:
        names += ["v0", "v1", "v2"]
    for nm in names:
        in_specs.append(_full(prm[nm].shape))
        args.append(prm[nm])
    n_tok_out = 7 if has_vres else 7
    out_shape = [jax.ShapeDtypeStruct((b, t, RW_WIDTH), F32)] * n_tok_out
    out_specs = [tok(RW_WIDTH, 0)] * n_tok_out
    out_shape.append(jax.ShapeDtypeStruct((b, RW_WIDTH, t), F32))
    out_specs.append(pl.BlockSpec((None, RW_WIDTH, tt), lambda bi, i: (bi, 0, i)))
    if not has_vres:
        out_shape.append(jax.ShapeDtypeStruct((b, t, RW_WIDTH), F32))
        out_specs.append(tok(RW_WIDTH, 0))
    return pl.pallas_call(
        functools.partial(_rwkv_prep_kernel, has_vres),
        grid=(b, t // tt),
        in_specs=in_specs,
        out_specs=out_specs,
        out_shape=out_shape,
        scratch_shapes=[pltpu.VMEM((1, W_RKV), F32), pltpu.VMEM((1, W_LORA), F32)],
        compiler_params=_cparams(("parallel", "arbitrary")),
        name="rwkv_prep",
    )(*args)


def _rwkv_rec_kernel(r_ref, w_ref, k_ref, a_ref, b_ref, vt_ref, eyg_ref, wsum_ref, yt_ref,
                     s_ref, y_acc, vlhs, srbuf):
    nb, tc = r_ref.shape[0], r_ref.shape[1]
    pairs = range(RW_PAIRS)
    g = SUBLANES

    @pl.when(pl.program_id(1) == 0)
    def _():
        s_ref[...] = jnp.zeros_like(s_ref)

    y_acc[...] = jnp.zeros_like(y_acc)
    for n in range(nb):
        for p in pairs:
            vlhs[n, RW_HEAD * p:RW_HEAD * (p + 1), :tc] = vt_ref[n, LANES * p:LANES * p + RW_HEAD, :]
            vlhs[n, RW_HEAD * p:RW_HEAD * (p + 1), tc:] = vt_ref[n, LANES * p + RW_HEAD:LANES * (p + 1), :]
    lane = lax.broadcasted_iota(jnp.int32, (RW_PAIRS * RW_HEAD, LANES), 1)

    def rows_of(t, i):
        return jnp.concatenate([jnp.broadcast_to(t[p][i:i + 1], (RW_HEAD, LANES)) for p in pairs], axis=0)

    def group(j8, states):
        base = pl.multiple_of(j8 * g, g)
        rows = pl.ds(base, g)
        v_cols, tiles = [], []
        for n in range(nb):
            v_e = pltpu.roll(vlhs[n, :, :tc], lax.rem(tc - base, tc), 1)
            v_o = pltpu.roll(vlhs[n, :, tc:], lax.rem(tc + g - base, tc), 1)
            v_sel = jnp.where(lane < g, v_e, jnp.where(lane < 2 * g, v_o, 0.0))
            v_res = v_sel - v_sel.astype(BF16).astype(F32)
            v_cols.append((v_sel + pltpu.roll(v_res, 2 * g, 1)).astype(BF16))
            tiles.append([[ref[n, rows, LANES * p:LANES * (p + 1)] for p in pairs]
                          for ref in (a_ref, w_ref, b_ref, k_ref, r_ref)])
        states = list(states)
        for i in range(g):
            for n in range(nb):
                s, t = states[n], tiles[n]
                lhs = jnp.concatenate([(s * rows_of(t[0], i)).astype(BF16), v_cols[n]], axis=1)
                res = jnp.dot(lhs, wsum_ref[i], preferred_element_type=F32)
                s = s * rows_of(t[1], i) + res[:, :LANES] * rows_of(t[2], i) + res[:, LANES:] * rows_of(t[3], i)
                srbuf[n, :, LANES * i:LANES * (i + 1)] = (s * rows_of(t[4], i)).astype(BF16)
                states[n] = s
        for n in range(nb):
            yg = jnp.dot(srbuf[n], eyg_ref[...], preferred_element_type=F32)
            y_acc[n, :, :tc] += pltpu.roll(yg[:, :tc], base, 1)
            y_acc[n, :, tc:] += pltpu.roll(yg[:, tc:], base, 1)
        return tuple(states)

    states = lax.fori_loop(0, tc // g, group, tuple(s_ref[n] for n in range(nb)))
    for n in range(nb):
        s_ref[n] = states[n]
        for p in pairs:
            yt_ref[n, LANES * p:LANES * p + RW_HEAD, :] = y_acc[n, RW_HEAD * p:RW_HEAD * (p + 1), :tc]
            yt_ref[n, LANES * p + RW_HEAD:LANES * (p + 1), :] = y_acc[n, RW_HEAD * p:RW_HEAD * (p + 1), tc:]


def _rwkv_rec(r, w, k, a, bb, vt, tc):
    b, t, c = r.shape
    assert tc == LANES
    g = SUBLANES
    nb = 2 if b % 2 == 0 else 1
    tok = pl.BlockSpec((nb, tc, c), lambda bi, i: (bi, i, 0))
    tr = pl.BlockSpec((nb, c, tc), lambda bi, i: (bi, 0, i))
    step = jnp.arange(g * LANES) // LANES
    key = jnp.arange(g * LANES) % LANES
    out = jnp.arange(2 * tc)
    e_yg = ((key[:, None] // RW_HEAD == out[None, :] // tc) & (step[:, None] == out[None, :] % tc)).astype(BF16)
    row = jnp.arange(2 * LANES)[None, :, None]
    col = jnp.arange(2 * LANES)[None, None, :]
    i = jnp.arange(g)[:, None, None]
    vrow = row - LANES
    w_sum = jnp.where(row < LANES, (col < LANES) & (row // RW_HEAD == col // RW_HEAD),
                      (col >= LANES) & (vrow < 4 * g) & (vrow % g == i)
                      & ((vrow // g) % 2 == (col - LANES) // RW_HEAD)).astype(BF16)
    rows = RW_PAIRS * RW_HEAD
    return pl.pallas_call(
        _rwkv_rec_kernel,
        grid=(b // nb, t // tc),
        in_specs=[tok, tok, tok, tok, tok, tr, _full(e_yg.shape), _full(w_sum.shape)],
        out_specs=tr,
        out_shape=jax.ShapeDtypeStruct((b, c, t), F32),
        scratch_shapes=[pltpu.VMEM((nb, rows, LANES), F32), pltpu.VMEM((nb, rows, 2 * tc), F32),
                        pltpu.VMEM((nb, rows, 2 * tc), F32), pltpu.VMEM((nb, rows, g * LANES), BF16)],
        compiler_params=_cparams(("parallel", "arbitrary")),
        name="rwkv_rec",
    )(r, w, k, a, bb, vt, e_yg, w_sum)


def _rwkv_post_kernel(yt_ref, bonus_ref, g_ref, lnw_ref, lnb_ref, e_ref, et_ref, wb_ref, o_ref):
    y = yt_ref[...].T
    e, et = e_ref[...], et_ref[...]
    mean = _xdot(_xdot(y, e), et) * (1.0 / RW_HEAD)
    yc = y - mean
    var = _xdot(_xdot(yc * yc, e), et) * (1.0 / RW_HEAD)
    y = yc * lax.rsqrt(var + RW_GN_EPS) * lnw_ref[...] + lnb_ref[...]
    y = (y + bonus_ref[...]) * g_ref[...]
    o_ref[...] = _bdot(y, wb_ref[...])


def _rwkv_post(yt, bonus, g, prm, tt):
    b, c, t = yt.shape
    tok = pl.BlockSpec((None, tt, c), lambda bi, i: (bi, i, 0))
    names = ["ln_w", "ln_b", "seg", "seg_t", "w_branch"]
    return pl.pallas_call(
        _rwkv_post_kernel,
        grid=(b, t // tt),
        in_specs=[pl.BlockSpec((None, c, tt), lambda bi, i: (bi, 0, i)), tok, tok] + [_full(prm[nm].shape) for nm in names],
        out_specs=pl.BlockSpec((None, tt, D_MODEL), lambda bi, i: (bi, i, 0)),
        out_shape=jax.ShapeDtypeStruct((b, t, D_MODEL), F32),
        compiler_params=_cparams(("parallel", "parallel")),
        name="rwkv_post",
    )(yt, bonus, g, *[prm[nm] for nm in names])


def _ssd_kernel(z_ref, xbc_ref, dt_ref, cw_ref, cb_ref, dtb_ref, alog_ref, dskip_ref, ng_ref, ex_ref, wb_ref,
                o_ref, buf, st_ref):
    L = SSD_CHUNK
    first = pl.program_id(1) == 0

    @pl.when(first)
    def _():
        buf[0:8, :] = jnp.zeros((8, SSD_CONV_CH), F32)
        st_ref[...] = jnp.zeros_like(st_ref)

    buf[8:8 + L, :] = xbc_ref[...]
    conv = cb_ref[...]
    for j in range(SSD_CONV):
        off = 8 - (SSD_CONV - 1) + j
        conv = conv + buf[off:off + L, :] * cw_ref[j:j + 1, :]
    buf[0:8, :] = buf[L:L + 8, :]
    xa = conv * _sigmoid(conv)
    xs = xa[:, :SSD_WIDTH]

    dt = _softplus(dt_ref[...] + dtb_ref[...])
    a = dt * (-jnp.exp(alog_ref[...]))
    row = lax.broadcasted_iota(jnp.int32, (L, L), 0)
    col = lax.broadcasted_iota(jnp.int32, (L, L), 1)
    causal = row >= col
    acs = _xdot(causal.astype(F32), a)
    acs_last = acs[L - 1:L, :]
    stacked = jnp.concatenate(
        [dt, jnp.exp(acs), jnp.exp(acs_last - acs), jnp.broadcast_to(jnp.exp(acs_last), (8, LANES))], axis=0)
    wide = _xdot(stacked, ex_ref[...])
    dt_x, ea_x, dte_x, el_x = wide[:L], wide[L:2 * L], wide[2 * L:3 * L], wide[3 * L:3 * L + 1]
    acs_t = acs.T
    xdt = xs * dt_x
    lane = lax.broadcasted_iota(jnp.int32, (L, LANES), 1)
    is_lo = lane < SSD_HEAD

    y_parts = []
    for g in range(SSD_GROUPS):
        bg = xa[:, SSD_WIDTH + SSD_STATE * g:SSD_WIDTH + SSD_STATE * (g + 1)]
        cg = xa[:, SSD_WIDTH + SSD_STATE * (SSD_GROUPS + g):SSD_WIDTH + SSD_STATE * (SSD_GROUPS + g + 1)]
        cb = lax.dot_general(cg.astype(BF16), bg.astype(BF16), (((1,), (1,)), ((), ())), preferred_element_type=F32)
        gl = slice(SSD_GW * g, SSD_GW * (g + 1))
        y_off = _bdot(cg, st_ref[g]) * ea_x[:, gl]
        diag = []
        for pp in range(SSD_HG // 2):
            acc = None
            xp = xdt[:, SSD_GW * g + LANES * pp:SSD_GW * g + LANES * (pp + 1)]
            for half in range(2):
                h = SSD_HG * g + 2 * pp + half
                seg = acs[:, h:h + 1] - acs_t[h:h + 1, :]
                lm = jnp.exp(jnp.where(causal, seg, -BIG))
                xh = jnp.where(is_lo if half == 0 else jnp.logical_not(is_lo), xp, 0.0)
                term = _bdot(lm * cb, xh)
                acc = term if acc is None else acc + term
            diag.append(acc)
        y_parts.append(jnp.concatenate(diag, axis=1) + y_off)
        st_ref[g] = st_ref[g] * el_x[:, gl] + _bdot(bg.T, xdt[:, gl] * dte_x[:, gl])
    y = jnp.concatenate(y_parts, axis=1)
    y = y + xs * dskip_ref[...]
    z = z_ref[...]
    y = y * (z * _sigmoid(z))
    normed = []
    for g in range(SSD_GROUPS):
        yg = y[:, SSD_GW * g:SSD_GW * (g + 1)]
        ms = jnp.mean(yg * yg, axis=-1, keepdims=True)
        normed.append(yg * lax.rsqrt(ms + NORM_EPS))
    y = jnp.concatenate(normed, axis=1) * ng_ref[...]
    o_ref[...] = _bdot(y, wb_ref[...])


def _ssd(proj3, prm):
    b, t, _ = proj3.shape
    L = SSD_CHUNK
    tok = lambda w, off: pl.BlockSpec((None, L, w), lambda bi, i: (bi, i, off // w))
    names = ["conv_w", "conv_b", "dt_bias", "a_log", "d_skip", "norm_g", "expand", "w_branch"]
    return pl.pallas_call(
        _ssd_kernel,
        grid=(b, t // L),
        in_specs=[tok(W_Z, OFF_Z), tok(W_XBC, OFF_XBC), tok(W_DT, OFF_DT)] + [_full(prm[nm].shape) for nm in names],
        out_specs=pl.BlockSpec((None, L, D_MODEL), lambda bi, i: (bi, i, 0)),
        out_shape=jax.ShapeDtypeStruct((b, t, D_MODEL), F32),
        scratch_shapes=[pltpu.VMEM((L + 8, SSD_CONV_CH), F32), pltpu.VMEM((SSD_GROUPS, SSD_STATE, SSD_GW), F32)],
        compiler_params=_cparams(("parallel", "arbitrary")),
        name="ssd",
    )(proj3, proj3, proj3, *[prm[nm] for nm in names])


def _mix_kernel(ya_ref, yb_ref, gate_ref, h_ref, wo_ref, gf_ref, wq_ref, keys_ref, h_o, ut_o, s_o):
    gt = gate_ref[...]
    mixed = _sigmoid(gt[:, :D_MODEL]) * ya_ref[...] + _sigmoid(gt[:, D_MODEL:]) * yb_ref[...]
    h = h_ref[...] + _bdot(mixed, wo_ref[...])
    h_o[...] = h
    ms = jnp.mean(h * h, axis=-1, keepdims=True)
    u = h * lax.rsqrt(ms + NORM_EPS) * gf_ref[...]
    ut_o[...] = u.T.astype(BF16)
    q = _bdot(u, wq_ref[...])
    for j in range(2 * PEER_HEADS):
        sl = slice(LANES * j, LANES * (j + 1))
        s_o[:, sl] = _bdot(q[:, sl], keys_ref[j])


def _mix(ya, yb, proj, h, prm, tm):
    n = h.shape[0]
    names = ["w_out", "norm_ffn_g", "w_q", "keys_t"]
    tok = lambda w, c: pl.BlockSpec((tm, w), lambda i: (i, c))
    return pl.pallas_call(
        _mix_kernel,
        grid=(n // tm,),
        in_specs=[tok(D_MODEL, 0), tok(D_MODEL, 0), tok(W_GATE, OFF_GATE // W_GATE), tok(D_MODEL, 0)]
        + [_full(prm[nm].shape) for nm in names],
        out_specs=[tok(D_MODEL, 0), pl.BlockSpec((D_MODEL, tm), lambda i: (0, i)), tok(2 * PEER_HEADS * PEER_NKEYS, 0)],
        out_shape=[jax.ShapeDtypeStruct((n, D_MODEL), F32), jax.ShapeDtypeStruct((D_MODEL, n), BF16),
                   jax.ShapeDtypeStruct((n, 2 * PEER_HEADS * PEER_NKEYS), F32)],
        compiler_params=_cparams(("parallel",)),
        name="mix",
    )(ya, yb, proj, h, *[prm[nm] for nm in names])


def _top_values(x, n, lane):
    tops = jnp.zeros((x.shape[0], LANES), F32)
    m = None
    for k in range(n):
        m = jnp.max(x, axis=-1, keepdims=True)
        tops = jnp.where(lane == k, m, tops)
        if k + 1 < n:
            x = jnp.where(x >= m, -BIG, x)
    return tops, m


def _peer_sel_kernel(s_ref, x1_ref, x2_ref, e1_o, t1_o, e2_o, d2_o):
    tm = s_ref.shape[0]
    lane = lax.broadcasted_iota(jnp.int32, (tm, LANES), 1)
    for h in range(PEER_HEADS):
        s1 = s_ref[:, 2 * LANES * h:2 * LANES * h + LANES]
        s2 = s_ref[:, 2 * LANES * h + LANES:2 * LANES * (h + 1)]
        top1, thr1 = _top_values(s1, PEER_TOPK, lane)
        top2, thr2 = _top_values(s2, PEER_TOPK, lane)
        cand = _xdot(top1, x1_ref[...]) + _xdot(top2, x2_ref[...])
        m0 = None
        zsum = None
        c_prev = None
        x = cand
        for k in range(PEER_TOPK + 1):
            m = jnp.max(x, axis=-1, keepdims=True)
            if k == 0:
                m0 = m
                zsum = jnp.ones_like(m)
            elif k < PEER_TOPK:
                zsum = zsum + jnp.exp(m - m0)
            if k == PEER_TOPK - 1:
                c_prev = m
            if k < PEER_TOPK:
                x = jnp.where(x >= m, -BIG, x)
        thr = 0.5 * (c_prev + m)
        in1 = s1 >= thr1
        in2 = s2 >= thr2
        m1 = jnp.max(s1, axis=-1, keepdims=True)
        m2 = jnp.max(s2, axis=-1, keepdims=True)
        e1 = jnp.where(in1, jnp.exp(s1 - m1), 0.0) / zsum
        e2 = jnp.where(in2, jnp.exp(s2 - m2), 0.0)
        e1_o[h] = e1.T
        t1_o[h] = jnp.where(in1, thr - s1, BIG).T
        e2_o[h] = e2.T
        d2_o[h] = jnp.where(in2, s2, -BIG).T


def _peer_sel(s, prm, tm):
    n = s.shape[0]
    out = jax.ShapeDtypeStruct((PEER_HEADS, PEER_NKEYS, n), F32)
    ospec = pl.BlockSpec((PEER_HEADS, PEER_NKEYS, tm), lambda i: (0, 0, i))
    return pl.pallas_call(
        _peer_sel_kernel,
        grid=(n // tm,),
        in_specs=[pl.BlockSpec((tm, s.shape[1]), lambda i: (i, 0)), _full(prm["cand1"].shape), _full(prm["cand2"].shape)],
        out_specs=[ospec] * 4,
        out_shape=[out] * 4,
        compiler_params=_cparams(("parallel",)),
        name="peer_sel",
    )(s, prm["cand1"], prm["cand2"])


def _gelu(x):
    return 0.5 * x * (1.0 + jnp.tanh(0.7978845608028654 * (x + 0.044715 * x * x * x)))


def _peer_ffn_kernel(final_g, ut_ref, down_ref, upt_ref, e1_ref, t1_ref, e2_ref, d2_ref, h_ref, *rest):
    if final_g:
        gfin_ref, o_ref, acc, act, hs = rest
    else:
        o_ref, acc, act, hs = rest
    j = pl.program_id(1)
    te, tl = act.shape
    ni = te // PEER_NKEYS

    @pl.when(j == 0)
    def _():
        acc[...] = jnp.zeros_like(acc)

    act[...] = jnp.dot(down_ref[...], ut_ref[...], preferred_element_type=F32)

    def first_key(ii, carry):
        rows = pl.ds(pl.multiple_of(ii * PEER_NKEYS, PEER_NKEYS), PEER_NKEYS)
        for c in range(tl // LANES):
            cl = slice(LANES * c, LANES * (c + 1))
            gmat = jnp.zeros((PEER_NKEYS, LANES), F32)
            for h in range(PEER_HEADS):
                t1 = t1_ref[ii, h:h + 1, cl]
                e1 = e1_ref[ii, h:h + 1, cl]
                gmat = gmat + jnp.where(d2_ref[h, :, cl] >= t1, e2_ref[h, :, cl], 0.0) * e1
            hs[rows, cl] = (gmat * _gelu(act[rows, cl])).astype(BF16)
        return carry

    lax.fori_loop(0, ni, first_key, 0)
    acc[...] += jnp.dot(upt_ref[...], hs[...], preferred_element_type=F32)

    @pl.when(j == pl.num_programs(1) - 1)
    def _():
        h = h_ref[...] + acc[...].T
        if final_g:
            ms = jnp.mean(h * h, axis=-1, keepdims=True)
            h = h * lax.rsqrt(ms + NORM_EPS) * gfin_ref[...]
        o_ref[...] = h


def _peer_ffn(ut, down, upt, e1, t1, e2, d2, h, final_g, tl, te):
    n = h.shape[0]
    sel = pl.BlockSpec((PEER_HEADS, PEER_NKEYS, tl), lambda i, j: (0, 0, i))
    sel1 = pl.BlockSpec((te // PEER_NKEYS, PEER_HEADS, tl), lambda i, j: (j, 0, i))
    in_specs = [
        pl.BlockSpec((D_MODEL, tl), lambda i, j: (0, i)),
        pl.BlockSpec((te, D_MODEL), lambda i, j: (j, 0)),
        pl.BlockSpec((D_MODEL, te), lambda i, j: (0, j)),
        sel1, sel1, sel, sel,
        pl.BlockSpec((tl, D_MODEL), lambda i, j: (i, 0)),
    ]
    args = [ut, down, upt, e1, t1, e2, d2, h]
    if final_g is not None:
        in_specs.append(_full(final_g.shape))
        args.append(final_g)
    return pl.pallas_call(
        functools.partial(_peer_ffn_kernel, final_g is not None),
        grid=(n // tl, PEER_EXPERTS // te),
        in_specs=in_specs,
        out_specs=pl.BlockSpec((tl, D_MODEL), lambda i, j: (i, 0)),
        out_shape=jax.ShapeDtypeStruct((n, D_MODEL), F32),
        scratch_shapes=[pltpu.VMEM((D_MODEL, tl), F32), pltpu.VMEM((te, tl), F32), pltpu.VMEM((te, tl), BF16)],
        compiler_params=_cparams(("parallel", "arbitrary")),
        name="peer_ffn",
    )(*args)


def _pad_rows(w, start, total):
    return jnp.zeros((total, w.shape[1]), w.dtype).at[start:start + w.shape[0]].set(w)


def _pad_lanes(v, total):
    return jnp.zeros((1, total), v.dtype).at[0, :v.shape[0]].set(v)


def _row(v):
    return v.reshape(1, -1)


def _constants():
    c = jnp.arange(RW_WIDTH)[:, None] // RW_HEAD == jnp.arange(LANES)[None, :]
    seg = c.astype(F32)
    expand = (jnp.arange(LANES)[:, None] == jnp.arange(SSD_WIDTH)[None, :] // SSD_HEAD).astype(F32)
    cc = jnp.arange(PEER_TOPK * PEER_TOPK)[None, :]
    kk = jnp.arange(LANES)[:, None]
    cand1 = (cc // PEER_TOPK == kk).astype(F32)
    cand2 = (cc % PEER_TOPK == kk).astype(F32)
    return seg, seg.T, expand, cand1, cand2


def kernel(x, norm_mix_g, w_in, rwkv_mu, rwkv_w0, rwkv_w2, rwkv_a0, rwkv_a2, rwkv_g2, rwkv_v0, rwkv_v1, rwkv_v2,
           rwkv_k_k, rwkv_k_a, rwkv_r_k, rwkv_ln_w, rwkv_ln_b, w_rwkv_branch, ssd_conv_w, ssd_conv_b, ssd_dt_bias,
           ssd_A_log, ssd_D, ssd_norm_g, w_ssd_branch, w_out, norm_ffn_g, peer_w_q, peer_sub_keys, peer_down, peer_up,
           norm_final_g):
    bsz, t, d = x.shape
    n = bsz * t
    depth = w_in.shape[0]
    seg, seg_t, expand, cand1, cand2 = _constants()
    h = x.reshape(n, d)
    v_first = None
    o_lora = 3 * RW_WIDTH
    o_z = o_lora + RW_LORA
    o_xbc = o_z + SSD_WIDTH
    o_dt = o_xbc + SSD_CONV_CH
    o_gate = o_dt + SSD_HEADS
    tt = min(256, t)
    for l in range(depth):
        wl = w_in[l]
        zpad = lambda k: jnp.zeros((d, k), wl.dtype)
        w_cat = jnp.concatenate([
            wl[:, :o_lora], wl[:, o_xbc:o_dt], wl[:, o_z:o_xbc], wl[:, o_gate:],
            wl[:, o_lora:o_z], zpad(RW_LORA_PAD - RW_LORA), wl[:, o_dt:o_gate], zpad(LANES - SSD_HEADS)], axis=1).astype(BF16)
        proj = _norm_matmul(h, _row(norm_mix_g[l]), w_cat, tm=min(512, n), tn=640)
        proj3 = proj.reshape(bsz, t, PROJ_PAD)

        mu = rwkv_mu[l]
        rw = {
            "mu1": _row(mu[:o_lora]), "mu2": _pad_lanes(mu[o_lora:], RW_LORA_PAD),
            "w0": _row(rwkv_w0[l]), "w2": _pad_rows(rwkv_w2[l], 0, RW_LORA_PAD).astype(BF16),
            "a0": _row(rwkv_a0[l]), "a2": _pad_rows(rwkv_a2[l], DECAY_LORA, RW_LORA_PAD).astype(BF16),
            "g2": _pad_rows(rwkv_g2[l], DECAY_LORA + ICLR_LORA, RW_LORA_PAD).astype(BF16),
            "k_k": _row(rwkv_k_k[l]), "k_a": _row(rwkv_k_a[l]), "r_k": _row(rwkv_r_k[l]),
            "seg": seg, "seg_t": seg_t,
            "ln_w": _row(rwkv_ln_w[l]), "ln_b": _row(rwkv_ln_b[l]), "w_branch": w_rwkv_branch[l].astype(BF16),
        }
        if l > 0:
            rw["v0"] = _row(rwkv_v0[l - 1])
            rw["v1"] = jnp.zeros((RW_WIDTH, VRES_PAD), F32).at[:, :VRES_LORA].set(rwkv_v1[l - 1]).astype(BF16)
            rw["v2"] = _pad_rows(rwkv_v2[l - 1], 0, VRES_PAD).astype(BF16)
        outs = _rwkv_prep(proj3, v_first, rw, tt)
        r_, w_, k_, a_, b_, g_, bonus, vt = outs[:8]
        if l == 0:
            v_first = outs[8]
        yt = _rwkv_rec(r_, w_, k_, a_, b_, vt, tc=min(LANES, t))
        y_a = _rwkv_post(yt, bonus, g_, rw, tt)

        sp = {
            "conv_w": ssd_conv_w[l], "conv_b": _row(ssd_conv_b[l]),
            "dt_bias": _pad_lanes(ssd_dt_bias[l], LANES), "a_log": _pad_lanes(ssd_A_log[l], LANES),
            "d_skip": _row(jnp.repeat(ssd_D[l], SSD_HEAD)), "norm_g": _row(ssd_norm_g[l]),
            "expand": expand, "w_branch": w_ssd_branch[l].astype(BF16),
        }
        y_b = _ssd(proj3, sp)

        pp = {
            "w_out": w_out[l].astype(BF16), "norm_ffn_g": _row(norm_ffn_g[l]), "w_q": peer_w_q[l].astype(BF16),
            "keys_t": jnp.swapaxes(peer_sub_keys[l].reshape(2 * PEER_HEADS, PEER_NKEYS, PEER_QDIM // 2), 1, 2).astype(BF16),
            "cand1": cand1, "cand2": cand2,
        }
        h, ut, s = _mix(y_a.reshape(n, d), y_b.reshape(n, d), proj, h, pp, tm=min(256, n))
        e1, t1, e2, d2 = _peer_sel(s, pp, tm=min(256, n))
        fin = _row(norm_final_g) if l == depth - 1 else None
        e1, t1 = jnp.swapaxes(e1, 0, 1), jnp.swapaxes(t1, 0, 1)
        h = _peer_ffn(ut, peer_down[l].astype(BF16), peer_up[l].T.astype(BF16), e1, t1, e2, d2, h, fin,
                      tl=min(512, n), te=2 * PEER_NKEYS)
    return h.reshape(bsz, t, d)
```

```python
import functools

import jax
import jax.numpy as jnp
from jax import lax
from jax.experimental import pallas as pl
from jax.experimental.pallas import tpu as pltpu

F32 = jnp.float32
BF16 = jnp.bfloat16
HI = lax.Precision.HIGHEST

LANES = 128
SUBLANES = 8
NORM_EPS = 1e-6

D_MODEL = 1024
RW_HEAD = 64
RW_WIDTH = D_MODEL
RW_HEADS = RW_WIDTH // RW_HEAD
RW_PAIRS = RW_WIDTH // LANES
DECAY_LORA, ICLR_LORA, GATE_LORA, VRES_LORA = 64, 64, 160, 32
RW_LORA = DECAY_LORA + ICLR_LORA + GATE_LORA
RW_LORA_PAD = 512
VRES_PAD = 128
RW_GN_EPS = 64e-5
REC_GROUP = 16
SSD_WIDTH = 2 * D_MODEL
SSD_HEAD = 64
SSD_HEADS = SSD_WIDTH // SSD_HEAD
SSD_STATE = 128
SSD_GROUPS = 4
SSD_GW = SSD_WIDTH // SSD_GROUPS
SSD_HG = SSD_HEADS // SSD_GROUPS
SSD_CONV = 4
SSD_CONV_CH = SSD_WIDTH + 2 * SSD_GROUPS * SSD_STATE
SSD_CHUNK = 128
PEER_HEADS = 8
PEER_NKEYS = 128
PEER_TOPK = 16
PEER_QDIM = 256
PEER_EXPERTS = PEER_NKEYS * PEER_NKEYS
BIG = 1e30

OFF_RKV, W_RKV = 0, 3 * RW_WIDTH
OFF_XBC, W_XBC = 3072, SSD_CONV_CH
OFF_Z, W_Z = 6144, SSD_WIDTH
OFF_GATE, W_GATE = 8192, 2 * D_MODEL
OFF_LORA, W_LORA = 10240, RW_LORA_PAD
OFF_DT, W_DT = 10752, LANES
PROJ_PAD = 10880

VMEM_LIMIT = 56 * 1024 * 1024


def _cparams(sem):
    return pltpu.CompilerParams(dimension_semantics=sem, vmem_limit_bytes=VMEM_LIMIT)


def _full(shape):
    n = len(shape)
    return pl.BlockSpec(shape, lambda *_: (0,) * n)


def _sigmoid(x):
    return 1.0 / (1.0 + jnp.exp(-x))


def _softplus(x):
    return jnp.maximum(x, 0.0) + jnp.log(1.0 + jnp.exp(-jnp.abs(x)))


def _bdot(a, b):
    return jnp.dot(a.astype(BF16), b.astype(BF16), preferred_element_type=F32)


def _xdot(a, b):
    return jnp.dot(a, b, precision=HI, preferred_element_type=F32)


def _norm_matmul_kernel(x_ref, g_ref, w_ref, o_ref, u_ref):
    @pl.when(pl.program_id(1) == 0)
    def _():
        x = x_ref[...]
        ms = jnp.mean(x * x, axis=-1, keepdims=True)
        u_ref[...] = (x * lax.rsqrt(ms + NORM_EPS) * g_ref[...]).astype(BF16)

    o_ref[...] = jnp.dot(u_ref[...], w_ref[...], preferred_element_type=F32)


def _norm_matmul(x, g, w, tm, tn):
    n, d = x.shape
    m = w.shape[1]
    return pl.pallas_call(
        _norm_matmul_kernel,
        grid=(n // tm, m // tn),
        in_specs=[
            pl.BlockSpec((tm, d), lambda i, j: (i, 0)),
            pl.BlockSpec((1, d), lambda i, j: (0, 0)),
            pl.BlockSpec((d, tn), lambda i, j: (0, j)),
        ],
        out_specs=pl.BlockSpec((tm, tn), lambda i, j: (i, j)),
        out_shape=jax.ShapeDtypeStruct((n, m), F32),
        scratch_shapes=[pltpu.VMEM((tm, d), BF16)],
        compiler_params=_cparams(("parallel", "arbitrary")),
        name="in_proj",
    )(x, g, w)


def _shift_mix(x, prev_ref, mu, first):
    rows = lax.broadcasted_iota(jnp.int32, x.shape, 0)
    prev = jnp.where(first, 0.0, prev_ref[...])
    xs = jnp.where(rows == 0, prev, pltpu.roll(x, 1, 0))
    prev_ref[...] = x[x.shape[0] - 1:, :]
    return x + (xs - x) * mu


def _rwkv_prep_kernel(has_vres, *refs):
    if has_vres:
        (prkv_ref, plora_ref, vf_ref, mu1_ref, mu2_ref, w0_ref, w2_ref, a0_ref, a2_ref, g2_ref,
         kk_ref, ka_ref, rk_ref, e_ref, et_ref, v0_ref, v1_ref, v2_ref,
         r_o, w_o, k_o, a_o, b_o, g_o, bonus_o, vt_o, prev1, prev2) = refs
    else:
        (prkv_ref, plora_ref, mu1_ref, mu2_ref, w0_ref, w2_ref, a0_ref, a2_ref, g2_ref,
         kk_ref, ka_ref, rk_ref, e_ref, et_ref,
         r_o, w_o, k_o, a_o, b_o, g_o, bonus_o, vt_o, v_o, prev1, prev2) = refs
    first = pl.program_id(1) == 0
    p = _shift_mix(prkv_ref[...], prev1, mu1_ref[...], first)
    lo = _shift_mix(plora_ref[...], prev2, mu2_ref[...], first)
    r = p[:, :RW_WIDTH]
    k = p[:, RW_WIDTH:2 * RW_WIDTH]
    v = p[:, 2 * RW_WIDTH:]
    w_log = -_softplus(-(w0_ref[...] + _bdot(jnp.tanh(lo), w2_ref[...]))) - 0.5
    decay = jnp.exp(-jnp.exp(w_log))
    a = _sigmoid(a0_ref[...] + _bdot(lo, a2_ref[...]))
    g = _bdot(_sigmoid(lo), g2_ref[...])
    if has_vres:
        mix = _sigmoid(v0_ref[...] + _bdot(_bdot(v, v1_ref[...]), v2_ref[...]))
        v = v + (vf_ref[...] - v) * mix
    else:
        v_o[...] = v
    e, et = e_ref[...], et_ref[...]
    kk = k * kk_ref[...]
    inv = lax.rsqrt(_xdot(kk * kk, e) + 1e-12)
    kk = kk * _xdot(inv, et)
    k2 = k * (1.0 + (a - 1.0) * ka_ref[...])
    rk = _xdot(_xdot(r * k2 * rk_ref[...], e), et)
    r_o[...] = r
    w_o[...] = decay
    k_o[...] = k2
    a_o[...] = -kk
    b_o[...] = kk * a
    g_o[...] = g
    bonus_o[...] = rk * v
    vt_o[...] = v.T


def _rwkv_prep(proj3, vfirst, prm, tt):
    b, t, _ = proj3.shape
    has_vres = vfirst is not None
    tok = lambda w, c: pl.BlockSpec((None, tt, w), lambda bi, i: (bi, i, c))
    in_specs = [tok(W_RKV, OFF_RKV // W_RKV), tok(W_LORA, OFF_LORA // W_LORA)]
    args = [proj3, proj3]
    if has_vres:
        in_specs.append(tok(RW_WIDTH, 0))
        args.append(vfirst)
    names = ["mu1", "mu2", "w0", "w2", "a0", "a2", "g2", "k_k", "k_a", "r_k", "seg", "seg_t"]
    if has_vres:
        names += ["v0", "v1", "v2"]
    for nm in names:
        in_specs.append(_full(prm[nm].shape))
        args.append(prm[nm])
    n_tok_out = 7
    out_shape = [jax.ShapeDtypeStruct((b, t, RW_WIDTH), F32)] * n_tok_out
    out_specs = [tok(RW_WIDTH, 0)] * n_tok_out
    out_shape.append(jax.ShapeDtypeStruct((b, RW_WIDTH, t), F32))
    out_specs.append(pl.BlockSpec((None, RW_WIDTH, tt), lambda bi, i: (bi, 0, i)))
    if not has_vres:
        out_shape.append(jax.ShapeDtypeStruct((b, t, RW_WIDTH), F32))
        out_specs.append(tok(RW_WIDTH, 0))
    return pl.pallas_call(
        functools.partial(_rwkv_prep_kernel, has_vres),
        grid=(b, t // tt),
        in_specs=in_specs,
        out_specs=out_specs,
        out_shape=out_shape,
        scratch_shapes=[pltpu.VMEM((1, W_RKV), F32), pltpu.VMEM((1, W_LORA), F32)],
        compiler_params=_cparams(("parallel", "arbitrary")),
        name="rwkv_prep",
    )(*args)


def _rwkv_rec_kernel(r_ref, w_ref, k_ref, a_ref, b_ref, vt_ref, eyg_ref, wsum_ref, yt_ref,
                     s_ref, y_acc, vlhs, srbuf):
    nb, tc = r_ref.shape[0], r_ref.shape[1]
    pairs = range(RW_PAIRS)
    g = REC_GROUP

    @pl.when(pl.program_id(1) == 0)
    def _():
        s_ref[...] = jnp.zeros_like(s_ref)

    y_acc[...] = jnp.zeros_like(y_acc)
    for n in range(nb):
        for p in pairs:
            vlhs[n, RW_HEAD * p:RW_HEAD * (p + 1), :tc] = vt_ref[n, LANES * p:LANES * p + RW_HEAD, :]
            vlhs[n, RW_HEAD * p:RW_HEAD * (p + 1), tc:] = vt_ref[n, LANES * p + RW_HEAD:LANES * (p + 1), :]
    lane = lax.broadcasted_iota(jnp.int32, (RW_PAIRS * RW_HEAD, LANES), 1)

    def rows_of(t, i):
        return jnp.concatenate([jnp.broadcast_to(t[p][i:i + 1], (RW_HEAD, LANES)) for p in pairs], axis=0)

    def group(j8, states):
        base = pl.multiple_of(j8 * g, g)
        rows = pl.ds(base, g)
        v_cols, tiles = [], []
        for n in range(nb):
            v_e = pltpu.roll(vlhs[n, :, :tc], lax.rem(tc - base, tc), 1)
            v_o = pltpu.roll(vlhs[n, :, tc:], lax.rem(tc + g - base, tc), 1)
            v_sel = jnp.where(lane < g, v_e, jnp.where(lane < 2 * g, v_o, 0.0))
            v_res = v_sel - v_sel.astype(BF16).astype(F32)
            v_cols.append((v_sel + pltpu.roll(v_res, 2 * g, 1)).astype(BF16))
            tiles.append([[ref[n, rows, LANES * p:LANES * (p + 1)] for p in pairs]
                          for ref in (a_ref, w_ref, b_ref, k_ref, r_ref)])
        states = list(states)
        for i in range(g):
            for n in range(nb):
                s, t = states[n], tiles[n]
                lhs = jnp.concatenate([(s * rows_of(t[0], i)).astype(BF16), v_cols[n]], axis=1)
                res = jnp.dot(lhs, wsum_ref[i], preferred_element_type=F32)
                s = s * rows_of(t[1], i) + res[:, :LANES] * rows_of(t[2], i) + res[:, LANES:] * rows_of(t[3], i)
                srbuf[n, :, LANES * i:LANES * (i + 1)] = (s * rows_of(t[4], i)).astype(BF16)
                states[n] = s
        for n in range(nb):
            yg = jnp.dot(srbuf[n], eyg_ref[...], preferred_element_type=F32)
            y_acc[n, :, :tc] += pltpu.roll(yg[:, :tc], base, 1)
            y_acc[n, :, tc:] += pltpu.roll(yg[:, tc:], base, 1)
        return tuple(states)

    states = lax.fori_loop(0, tc // g, group, tuple(s_ref[n] for n in range(nb)))
    for n in range(nb):
        s_ref[n] = states[n]
        for p in pairs:
            yt_ref[n, LANES * p:LANES * p + RW_HEAD, :] = y_acc[n, RW_HEAD * p:RW_HEAD * (p + 1), :tc]
            yt_ref[n, LANES * p + RW_HEAD:LANES * (p + 1), :] = y_acc[n, RW_HEAD * p:RW_HEAD * (p + 1), tc:]


def _rwkv_rec(r, w, k, a, bb, vt, tc):
    b, t, c = r.shape
    assert tc == LANES
    g = REC_GROUP
    nb = 2 if b % 2 == 0 else 1
    tok = pl.BlockSpec((nb, tc, c), lambda bi, i: (bi, i, 0))
    tr = pl.BlockSpec((nb, c, tc), lambda bi, i: (bi, 0, i))
    step = jnp.arange(g * LANES) // LANES
    key = jnp.arange(g * LANES) % LANES
    out = jnp.arange(2 * tc)
    e_yg = ((key[:, None] // RW_HEAD == out[None, :] // tc) & (step[:, None] == out[None, :] % tc)).astype(BF16)
    row = jnp.arange(2 * LANES)[None, :, None]
    col = jnp.arange(2 * LANES)[None, None, :]
    i = jnp.arange(g)[:, None, None]
    vrow = row - LANES
    w_sum = jnp.where(row < LANES, (col < LANES) & (row // RW_HEAD == col // RW_HEAD),
                      (col >= LANES) & (vrow < 4 * g) & (vrow % g == i)
                      & ((vrow // g) % 2 == (col - LANES) // RW_HEAD)).astype(BF16)
    rows = RW_PAIRS * RW_HEAD
    return pl.pallas_call(
        _rwkv_rec_kernel,
        grid=(b // nb, t // tc),
        in_specs=[tok, tok, tok, tok, tok, tr, _full(e_yg.shape), _full(w_sum.shape)],
        out_specs=tr,
        out_shape=jax.ShapeDtypeStruct((b, c, t), F32),
        scratch_shapes=[pltpu.VMEM((nb, rows, LANES), F32), pltpu.VMEM((nb, rows, 2 * tc), F32),
                        pltpu.VMEM((nb, rows, 2 * tc), F32), pltpu.VMEM((nb, rows, g * LANES), BF16)],
        compiler_params=_cparams(("parallel", "arbitrary")),
        name="rwkv_rec",
    )(r, w, k, a, bb, vt, e_yg, w_sum)


def _rwkv_post_kernel(yt_ref, bonus_ref, g_ref, lnw_ref, lnb_ref, e_ref, et_ref, wb_ref, o_ref):
    y = yt_ref[...].T
    e, et = e_ref[...], et_ref[...]
    mean = _xdot(_xdot(y, e), et) * (1.0 / RW_HEAD)
    yc = y - mean
    var = _xdot(_xdot(yc * yc, e), et) * (1.0 / RW_HEAD)
    y = yc * lax.rsqrt(var + RW_GN_EPS) * lnw_ref[...] + lnb_ref[...]
    y = (y + bonus_ref[...]) * g_ref[...]
    o_ref[...] = _bdot(y, wb_ref[...])


def _rwkv_post(yt, bonus, g, prm, tt):
    b, c, t = yt.shape
    tok = pl.BlockSpec((None, tt, c), lambda bi, i: (bi, i, 0))
    names = ["ln_w", "ln_b", "seg", "seg_t", "w_branch"]
    return pl.pallas_call(
        _rwkv_post_kernel,
        grid=(b, t // tt),
        in_specs=[pl.BlockSpec((None, c, tt), lambda bi, i: (bi, 0, i)), tok, tok] + [_full(prm[nm].shape) for nm in names],
        out_specs=pl.BlockSpec((None, tt, D_MODEL), lambda bi, i: (bi, i, 0)),
        out_shape=jax.ShapeDtypeStruct((b, t, D_MODEL), F32),
        compiler_params=_cparams(("parallel", "parallel")),
        name="rwkv_post",
    )(yt, bonus, g, *[prm[nm] for nm in names])


def _ssd_kernel(z_ref, xbc_ref, dt_ref, cw_ref, cb_ref, dtb_ref, alog_ref, dskip_ref, ng_ref, ex_ref, wb_ref,
                o_ref, buf, st_ref):
    L = SSD_CHUNK
    first = pl.program_id(1) == 0

    @pl.when(first)
    def _():
        buf[0:8, :] = jnp.zeros((8, SSD_CONV_CH), F32)
        st_ref[...] = jnp.zeros_like(st_ref)

    buf[8:8 + L, :] = xbc_ref[...]
    conv = cb_ref[...]
    for j in range(SSD_CONV):
        off = 8 - (SSD_CONV - 1) + j
        conv = conv + buf[off:off + L, :] * cw_ref[j:j + 1, :]
    buf[0:8, :] = buf[L:L + 8, :]
    xa = conv * _sigmoid(conv)
    xs = xa[:, :SSD_WIDTH]

    dt = _softplus(dt_ref[...] + dtb_ref[...])
    a = dt * (-jnp.exp(alog_ref[...]))
    row = lax.broadcasted_iota(jnp.int32, (L, L), 0)
    col = lax.broadcasted_iota(jnp.int32, (L, L), 1)
    causal = row >= col
    acs = _xdot(causal.astype(F32), a)
    acs_last = acs[L - 1:L, :]
    stacked = jnp.concatenate(
        [dt, jnp.exp(acs), jnp.exp(acs_last - acs), jnp.broadcast_to(jnp.exp(acs_last), (8, LANES))], axis=0)
    wide = _xdot(stacked, ex_ref[...])
    dt_x, ea_x, dte_x, el_x = wide[:L], wide[L:2 * L], wide[2 * L:3 * L], wide[3 * L:3 * L + 1]
    acs_t = acs.T
    xdt = xs * dt_x
    lane = lax.broadcasted_iota(jnp.int32, (L, LANES), 1)
    is_lo = lane < SSD_HEAD

    y_parts = []
    for g in range(SSD_GROUPS):
        bg = xa[:, SSD_WIDTH + SSD_STATE * g:SSD_WIDTH + SSD_STATE * (g + 1)]
        cg = xa[:, SSD_WIDTH + SSD_STATE * (SSD_GROUPS + g):SSD_WIDTH + SSD_STATE * (SSD_GROUPS + g + 1)]
        cb = lax.dot_general(cg.astype(BF16), bg.astype(BF16), (((1,), (1,)), ((), ())), preferred_element_type=F32)
        gl = slice(SSD_GW * g, SSD_GW * (g + 1))
        y_off = _bdot(cg, st_ref[g]) * ea_x[:, gl]
        diag = []
        for pp in range(SSD_HG // 2):
            acc = None
            xp = xdt[:, SSD_GW * g + LANES * pp:SSD_GW * g + LANES * (pp + 1)]
            for half in range(2):
                h = SSD_HG * g + 2 * pp + half
                seg = acs[:, h:h + 1] - acs_t[h:h + 1, :]
                lm = jnp.exp(jnp.where(causal, seg, -BIG))
                xh = jnp.where(is_lo if half == 0 else jnp.logical_not(is_lo), xp, 0.0)
                term = _bdot(lm * cb, xh)
                acc = term if acc is None else acc + term
            diag.append(acc)
        y_parts.append(jnp.concatenate(diag, axis=1) + y_off)
        st_ref[g] = st_ref[g] * el_x[:, gl] + _bdot(bg.T, xdt[:, gl] * dte_x[:, gl])
    y = jnp.concatenate(y_parts, axis=1)
    y = y + xs * dskip_ref[...]
    z = z_ref[...]
    y = y * (z * _sigmoid(z))
    normed = []
    for g in range(SSD_GROUPS):
        yg = y[:, SSD_GW * g:SSD_GW * (g + 1)]
        ms = jnp.mean(yg * yg, axis=-1, keepdims=True)
        normed.append(yg * lax.rsqrt(ms + NORM_EPS))
    y = jnp.concatenate(normed, axis=1) * ng_ref[...]
    o_ref[...] = _bdot(y, wb_ref[...])


def _ssd(proj3, prm):
    b, t, _ = proj3.shape
    L = SSD_CHUNK
    tok = lambda w, off: pl.BlockSpec((None, L, w), lambda bi, i: (bi, i, off // w))
    names = ["conv_w", "conv_b", "dt_bias", "a_log", "d_skip", "norm_g", "expand", "w_branch"]
    return pl.pallas_call(
        _ssd_kernel,
        grid=(b, t // L),
        in_specs=[tok(W_Z, OFF_Z), tok(W_XBC, OFF_XBC), tok(W_DT, OFF_DT)] + [_full(prm[nm].shape) for nm in names],
        out_specs=pl.BlockSpec((None, L, D_MODEL), lambda bi, i: (bi, i, 0)),
        out_shape=jax.ShapeDtypeStruct((b, t, D_MODEL), F32),
        scratch_shapes=[pltpu.VMEM((L + 8, SSD_CONV_CH), F32), pltpu.VMEM((SSD_GROUPS, SSD_STATE, SSD_GW), F32)],
        compiler_params=_cparams(("parallel", "arbitrary")),
        name="ssd",
    )(proj3, proj3, proj3, *[prm[nm] for nm in names])


def _mix_kernel(ya_ref, yb_ref, gate_ref, h_ref, wo_ref, gf_ref, h_o, ut_o):
    gt = gate_ref[...]
    mixed = _sigmoid(gt[:, :D_MODEL]) * ya_ref[...] + _sigmoid(gt[:, D_MODEL:]) * yb_ref[...]
    h = h_ref[...] + _bdot(mixed, wo_ref[...])
    h_o[...] = h
    ms = jnp.mean(h * h, axis=-1, keepdims=True)
    u = h * lax.rsqrt(ms + NORM_EPS) * gf_ref[...]
    ut_o[...] = u.T.astype(BF16)


def _mix(ya, yb, proj, h, prm, tm):
    n = h.shape[0]
    names = ["w_out", "norm_ffn_g"]
    tok = lambda w, c: pl.BlockSpec((tm, w), lambda i: (i, c))
    return pl.pallas_call(
        _mix_kernel,
        grid=(n // tm,),
        in_specs=[tok(D_MODEL, 0), tok(D_MODEL, 0), tok(W_GATE, OFF_GATE // W_GATE), tok(D_MODEL, 0)]
        + [_full(prm[nm].shape) for nm in names],
        out_specs=[tok(D_MODEL, 0), pl.BlockSpec((D_MODEL, tm), lambda i: (0, i))],
        out_shape=[jax.ShapeDtypeStruct((n, D_MODEL), F32), jax.ShapeDtypeStruct((D_MODEL, n), BF16)],
        compiler_params=_cparams(("parallel",)),
        name="mix",
    )(ya, yb, proj, h, *[prm[nm] for nm in names])


def _top_rows(x, n, want_rank):
    tm = x.shape[1]
    row_n = lax.broadcasted_iota(jnp.int32, (n, tm), 0)
    tops = jnp.zeros((n, tm), F32)
    rank = jnp.full(x.shape, float(n), F32) if want_rank else None
    m = None
    for k in range(n):
        m = jnp.max(x, axis=0, keepdims=True)
        tops = jnp.where(row_n == k, m, tops)
        hit = x >= m
        if want_rank:
            rank = jnp.where(hit, float(k), rank)
        if k + 1 < n:
            x = jnp.where(hit, -BIG, x)
    return tops, m, rank


def _peer_sel_kernel(ut_ref, wq_ref, keys_ref, e1_o, cnt_o, e2_o, r2_o):
    k = PEER_TOPK
    qt = jnp.dot(wq_ref[...], ut_ref[...], preferred_element_type=F32)
    for h in range(PEER_HEADS):
        s = []
        for c in range(2):
            j = 2 * h + c
            s.append(_bdot(keys_ref[j], qt[LANES * j:LANES * (j + 1), :]))
        s1, s2 = s
        top1, thr1, _ = _top_rows(s1, k, False)
        top2, _, rank2 = _top_rows(s2, k, True)
        in1 = s1 >= thr1
        in2 = rank2 < float(k)
        row8 = lax.broadcasted_iota(jnp.int32, (SUBLANES, s1.shape[1]), 0)
        groups = [top1[0:1, :] + top2]
        for k1 in range(1, SUBLANES):
            width = k // (k1 + 1)
            groups.append(jnp.where(row8 < width, top1[k1:k1 + 1, :] + top2[:SUBLANES, :], -BIG))
        groups.append(top1[SUBLANES:, :] + top2[0:1, :])
        x = jnp.concatenate(groups, axis=0)
        m0 = zsum = m = None
        for i in range(k):
            m = jnp.max(x, axis=0, keepdims=True)
            if i == 0:
                m0, zsum = m, jnp.ones_like(m)
            else:
                zsum = zsum + jnp.exp(m - m0)
            if i + 1 < k:
                x = jnp.where(x >= m, -BIG, x)
        thr = m
        cnt = jnp.zeros_like(s1)
        for k2 in range(k):
            cnt = cnt + jnp.where(s1 + top2[k2:k2 + 1, :] >= thr, 1.0, 0.0)
        e1_o[h] = jnp.where(in1, jnp.exp(s1 - top1[0:1, :]), 0.0) / zsum
        cnt_o[h] = jnp.where(in1, cnt, 0.0)
        e2_o[h] = pltpu.bitcast(jnp.where(in2, jnp.exp(s2 - top2[0:1, :]), 0.0).astype(BF16), jnp.uint32)
        r2_o[h] = pltpu.bitcast(rank2.astype(BF16), jnp.uint32)


def _peer_sel(ut, prm, tm):
    n = ut.shape[1]
    shp = (PEER_HEADS, PEER_NKEYS, n)
    ospec = pl.BlockSpec((PEER_HEADS, PEER_NKEYS, tm), lambda i: (0, 0, i))
    pshp = (PEER_HEADS, PEER_NKEYS // 2, n)
    pspec = pl.BlockSpec((PEER_HEADS, PEER_NKEYS // 2, tm), lambda i: (0, 0, i))
    return pl.pallas_call(
        _peer_sel_kernel,
        grid=(n // tm,),
        in_specs=[pl.BlockSpec((D_MODEL, tm), lambda i: (0, i)), _full(prm["wq_t"].shape), _full(prm["keys"].shape)],
        out_specs=[ospec, ospec, pspec, pspec],
        out_shape=[jax.ShapeDtypeStruct(shp, F32), jax.ShapeDtypeStruct(shp, F32),
                   jax.ShapeDtypeStruct(pshp, jnp.uint32), jax.ShapeDtypeStruct(pshp, jnp.uint32)],
        compiler_params=_cparams(("parallel",)),
        name="peer_sel",
    )(ut, prm["wq_t"], prm["keys"])


def _gated_gelu(x, gate):
    c = 0.7978845608028654
    t = jnp.tanh(x * (c + (c * 0.044715) * (x * x)))
    hx = x.astype(BF16) * 0.5
    return (hx + hx * t.astype(BF16)) * gate


def _peer_ffn_kernel(final_g, ut_ref, down_ref, upt_ref, uptp_ref, e1_ref, cnt_ref, e2_ref, r2_ref, h_ref, *rest):
    if final_g:
        gfin_ref, o_ref, *scr = rest
    else:
        o_ref, *scr = rest
    nh = len(scr) // 4
    acc, act, gbuf, hs = scr[:nh], scr[nh:2 * nh], scr[2 * nh:3 * nh], scr[3 * nh:]
    j = pl.program_id(1)
    te, th = act[0].shape
    sub = 2 * SUBLANES

    @pl.when(j == 0)
    def _():
        for a in acc:
            a[...] = jnp.zeros_like(a)
        hs[nh - 1][...] = jnp.zeros_like(hs[nh - 1])

    kb = 2
    blk = kb * PEER_NKEYS
    nblk = te // blk
    nm = D_MODEL // blk
    sbt = 4

    def project_down(hf, b):
        rows = slice(blk * b, blk * (b + 1))
        act[hf][rows, :] = jnp.dot(down_ref[rows, :], ut_ref[:, th * hf:th * (hf + 1)], preferred_element_type=F32)

    def build_gates(hf, b):
        for c in range(th // LANES):
            cl = slice(th * hf + LANES * c, th * hf + LANES * (c + 1))
            for s0 in range(0, PEER_NKEYS // sub, sbt):
                gate = [[jnp.zeros((sub, LANES), BF16) for _ in range(sbt)] for _ in range(kb)]
                for h in range(PEER_HEADS):
                    r2 = [pltpu.bitcast(r2_ref[h, SUBLANES * (s0 + s):SUBLANES * (s0 + s + 1), cl], BF16) for s in range(sbt)]
                    e2 = [pltpu.bitcast(e2_ref[h, SUBLANES * (s0 + s):SUBLANES * (s0 + s + 1), cl], BF16) for s in range(sbt)]
                    for q in range(kb):
                        ii = kb * b + q
                        cnt = jnp.broadcast_to(cnt_ref[ii, h:h + 1, cl], (sub, LANES)).astype(BF16)
                        e1 = jnp.broadcast_to(e1_ref[ii, h:h + 1, cl], (sub, LANES)).astype(BF16)
                        for s in range(sbt):
                            gate[q][s] = gate[q][s] + jnp.where(r2[s] < cnt, e2[s], 0.0) * e1
                for q in range(kb):
                    for s in range(sbt):
                        gw = (blk * b + PEER_NKEYS * q + sub * (s0 + s)) // 2
                        gbuf[hf][gw:gw + SUBLANES, LANES * c:LANES * (c + 1)] = pltpu.bitcast(gate[q][s], jnp.uint32)

    def activate(hf, b):
        for r0 in range(blk * b, blk * (b + 1), sub):
            for c in range(th // LANES):
                cl = slice(LANES * c, LANES * (c + 1))
                gate = pltpu.bitcast(gbuf[hf][r0 // 2:r0 // 2 + SUBLANES, cl], BF16)
                hs[hf][r0:r0 + sub, cl] = _gated_gelu(act[hf][r0:r0 + sub, cl], gate)

    def project_up(w_ref, hf, m):
        rows = slice(blk * m, blk * (m + 1))
        acc[hf][rows, :] += jnp.dot(w_ref[rows, :], hs[hf][...], preferred_element_type=F32)

    for hf in range(nh):
        for b in range(nblk):
            project_down(hf, b)
            for m in range(b * nm // nblk, (b + 1) * nm // nblk):
                if hf == 0:
                    project_up(uptp_ref, nh - 1, m)
                else:
                    project_up(upt_ref, hf - 1, m)
            build_gates(hf, b)
            if b > 0:
                activate(hf, b - 1)
        activate(hf, nblk - 1)

    @pl.when(j == pl.num_programs(1) - 1)
    def _():
        for m in range(nm):
            project_up(upt_ref, nh - 1, m)
        for hf in range(nh):
            h = h_ref[th * hf:th * (hf + 1), :] + acc[hf][...].T
            if final_g:
                ms = jnp.mean(h * h, axis=-1, keepdims=True)
                h = h * lax.rsqrt(ms + NORM_EPS) * gfin_ref[...]
            o_ref[th * hf:th * (hf + 1), :] = h


def _peer_ffn(ut, down, upt, e1, cnt, e2, r2, h, final_g, tl, te):
    n = h.shape[0]
    nh = 2 if tl % (2 * LANES) == 0 else 1
    sel = pl.BlockSpec((PEER_HEADS, PEER_NKEYS // 2, tl), lambda i, j: (0, 0, i))
    sel1 = pl.BlockSpec((te // PEER_NKEYS, PEER_HEADS, tl), lambda i, j: (j, 0, i))
    in_specs = [
        pl.BlockSpec((D_MODEL, tl), lambda i, j: (0, i)),
        pl.BlockSpec((te, D_MODEL), lambda i, j: (j, 0)),
        pl.BlockSpec((D_MODEL, te), lambda i, j: (0, j)),
        pl.BlockSpec((D_MODEL, te), lambda i, j: (0, jnp.maximum(j - 1, 0))),
        sel1, sel1, sel, sel,
        pl.BlockSpec((tl, D_MODEL), lambda i, j: (i, 0)),
    ]
    args = [ut, down, upt, upt, e1, cnt, e2, r2, h]
    if final_g is not None:
        in_specs.append(_full(final_g.shape))
        args.append(final_g)
    return pl.pallas_call(
        functools.partial(_peer_ffn_kernel, final_g is not None),
        grid=(n // tl, PEER_EXPERTS // te),
        in_specs=in_specs,
        out_specs=pl.BlockSpec((tl, D_MODEL), lambda i, j: (i, 0)),
        out_shape=jax.ShapeDtypeStruct((n, D_MODEL), F32),
        scratch_shapes=([pltpu.VMEM((D_MODEL, tl // nh), F32)] * nh + [pltpu.VMEM((te, tl // nh), F32)] * nh
                        + [pltpu.VMEM((te // 2, tl // nh), jnp.uint32)] * nh + [pltpu.VMEM((te, tl // nh), BF16)] * nh),
        compiler_params=_cparams(("parallel", "arbitrary")),
        name="peer_ffn",
    )(*args)


def _pad_rows(w, start, total):
    return jnp.zeros((total, w.shape[1]), w.dtype).at[start:start + w.shape[0]].set(w)


def _pad_lanes(v, total):
    return jnp.zeros((1, total), v.dtype).at[0, :v.shape[0]].set(v)


def _row(v):
    return v.reshape(1, -1)


def _constants():
    c = jnp.arange(RW_WIDTH)[:, None] // RW_HEAD == jnp.arange(LANES)[None, :]
    seg = c.astype(F32)
    expand = (jnp.arange(LANES)[:, None] == jnp.arange(SSD_WIDTH)[None, :] // SSD_HEAD).astype(F32)
    return seg, seg.T, expand


def kernel(x, norm_mix_g, w_in, rwkv_mu, rwkv_w0, rwkv_w2, rwkv_a0, rwkv_a2, rwkv_g2, rwkv_v0, rwkv_v1, rwkv_v2,
           rwkv_k_k, rwkv_k_a, rwkv_r_k, rwkv_ln_w, rwkv_ln_b, w_rwkv_branch, ssd_conv_w, ssd_conv_b, ssd_dt_bias,
           ssd_A_log, ssd_D, ssd_norm_g, w_ssd_branch, w_out, norm_ffn_g, peer_w_q, peer_sub_keys, peer_down, peer_up,
           norm_final_g):
    bsz, t, d = x.shape
    n = bsz * t
    depth = w_in.shape[0]
    seg, seg_t, expand = _constants()
    h = x.reshape(n, d)
    v_first = None
    o_lora = 3 * RW_WIDTH
    o_z = o_lora + RW_LORA
    o_xbc = o_z + SSD_WIDTH
    o_dt = o_xbc + SSD_CONV_CH
    o_gate = o_dt + SSD_HEADS
    tt = min(256, t)
    for l in range(depth):
        wl = w_in[l]
        zpad = lambda k: jnp.zeros((d, k), wl.dtype)
        w_cat = jnp.concatenate([
            wl[:, :o_lora], wl[:, o_xbc:o_dt], wl[:, o_z:o_xbc], wl[:, o_gate:],
            wl[:, o_lora:o_z], zpad(RW_LORA_PAD - RW_LORA), wl[:, o_dt:o_gate], zpad(LANES - SSD_HEADS)], axis=1).astype(BF16)
        proj = _norm_matmul(h, _row(norm_mix_g[l]), w_cat, tm=min(512, n), tn=640)
        proj3 = proj.reshape(bsz, t, PROJ_PAD)

        mu = rwkv_mu[l]
        rw = {
            "mu1": _row(mu[:o_lora]), "mu2": _pad_lanes(mu[o_lora:], RW_LORA_PAD),
            "w0": _row(rwkv_w0[l]), "w2": _pad_rows(rwkv_w2[l], 0, RW_LORA_PAD).astype(BF16),
            "a0": _row(rwkv_a0[l]), "a2": _pad_rows(rwkv_a2[l], DECAY_LORA, RW_LORA_PAD).astype(BF16),
            "g2": _pad_rows(rwkv_g2[l], DECAY_LORA + ICLR_LORA, RW_LORA_PAD).astype(BF16),
            "k_k": _row(rwkv_k_k[l]), "k_a": _row(rwkv_k_a[l]), "r_k": _row(rwkv_r_k[l]),
            "seg": seg, "seg_t": seg_t,
            "ln_w": _row(rwkv_ln_w[l]), "ln_b": _row(rwkv_ln_b[l]), "w_branch": w_rwkv_branch[l].astype(BF16),
        }
        if l > 0:
            rw["v0"] = _row(rwkv_v0[l - 1])
            rw["v1"] = jnp.zeros((RW_WIDTH, VRES_PAD), F32).at[:, :VRES_LORA].set(rwkv_v1[l - 1]).astype(BF16)
            rw["v2"] = _pad_rows(rwkv_v2[l - 1], 0, VRES_PAD).astype(BF16)
        outs = _rwkv_prep(proj3, v_first, rw, tt)
        r_, w_, k_, a_, b_, g_, bonus, vt = outs[:8]
        if l == 0:
            v_first = outs[8]
        yt = _rwkv_rec(r_, w_, k_, a_, b_, vt, tc=min(LANES, t))
        y_a = _rwkv_post(yt, bonus, g_, rw, tt)

        sp = {
            "conv_w": ssd_conv_w[l], "conv_b": _row(ssd_conv_b[l]),
            "dt_bias": _pad_lanes(ssd_dt_bias[l], LANES), "a_log": _pad_lanes(ssd_A_log[l], LANES),
            "d_skip": _row(jnp.repeat(ssd_D[l], SSD_HEAD)), "norm_g": _row(ssd_norm_g[l]),
            "expand": expand, "w_branch": w_ssd_branch[l].astype(BF16),
        }
        y_b = _ssd(proj3, sp)

        pp = {
            "w_out": w_out[l].astype(BF16), "norm_ffn_g": _row(norm_ffn_g[l]), "wq_t": peer_w_q[l].T.astype(BF16),
            "keys": peer_sub_keys[l].reshape(2 * PEER_HEADS, PEER_NKEYS, PEER_QDIM // 2).astype(BF16),
        }
        h, ut = _mix(y_a.reshape(n, d), y_b.reshape(n, d), proj, h, pp, tm=min(256, n))
        e1, cnt, e2, r2 = _peer_sel(ut, pp, tm=min(256, n))
        fin = _row(norm_final_g) if l == depth - 1 else None
        e1, cnt = jnp.swapaxes(e1, 0, 1), jnp.swapaxes(cnt, 0, 1)
        h = _peer_ffn(ut, peer_down[l].astype(BF16), peer_up[l].T.astype(BF16), e1, cnt, e2, r2, h, fin,
                      tl=min(512, n), te=8 * PEER_NKEYS)
    return h.reshape(bsz, t, d)
```

```python
import functools

import jax
import jax.numpy as jnp
from jax import lax
from jax.experimental import pallas as pl
from jax.experimental.pallas import tpu as pltpu

F32 = jnp.float32
BF16 = jnp.bfloat16

LANES = 128
SUBLANES = 8
NORM_EPS = 1e-6

D_MODEL = 1024
RW_HEAD = 64
RW_WIDTH = D_MODEL
RW_HEADS = RW_WIDTH // RW_HEAD
RW_PAIRS = RW_WIDTH // LANES
DECAY_LORA, ICLR_LORA, GATE_LORA, VRES_LORA = 64, 64, 160, 32
RW_LORA = DECAY_LORA + ICLR_LORA + GATE_LORA
RW_LORA_PAD = 512
VRES_PAD = 128
RW_GN_EPS = 64e-5
REC_CHUNK = 16
SSD_WIDTH = 2 * D_MODEL
SSD_HEAD = 64
SSD_HEADS = SSD_WIDTH // SSD_HEAD
SSD_STATE = 128
SSD_GROUPS = 4
SSD_GW = SSD_WIDTH // SSD_GROUPS
SSD_HG = SSD_HEADS // SSD_GROUPS
SSD_CONV = 4
SSD_CONV_CH = SSD_WIDTH + 2 * SSD_GROUPS * SSD_STATE
SSD_CHUNK = 128
PEER_HEADS = 8
PEER_NKEYS = 128
PEER_TOPK = 16
PEER_QDIM = 256
PEER_EXPERTS = PEER_NKEYS * PEER_NKEYS
BIG = 1e30

OFF_RKV, W_RKV = 0, 3 * RW_WIDTH
OFF_XBC, W_XBC = 3072, SSD_CONV_CH
OFF_Z, W_Z = 6144, SSD_WIDTH
OFF_GATE, W_GATE = 8192, 2 * D_MODEL
OFF_LORA, W_LORA = 10240, RW_LORA_PAD
OFF_DT, W_DT = 10752, LANES
PROJ_PAD = 10880

VMEM_LIMIT = 56 * 1024 * 1024


def _cparams(sem):
    return pltpu.CompilerParams(dimension_semantics=sem, vmem_limit_bytes=VMEM_LIMIT)


def _full(shape):
    n = len(shape)
    return pl.BlockSpec(shape, lambda *_: (0,) * n)


def _sigmoid(x):
    return 1.0 / (1.0 + jnp.exp(-x))


def _softplus(x):
    return jnp.maximum(x, 0.0) + jnp.log(1.0 + jnp.exp(-jnp.abs(x)))


def _bdot(a, b):
    return jnp.dot(a.astype(BF16), b.astype(BF16), preferred_element_type=F32)


def _split(x):
    hi = x.astype(BF16)
    return hi, (x - hi.astype(F32)).astype(BF16)


def _pick_dot(x, sel):
    hi, lo = _split(x)
    return jnp.dot(hi, sel, preferred_element_type=F32) + jnp.dot(lo, sel, preferred_element_type=F32)


def _pick_dot_left(sel, x):
    hi, lo = _split(x)
    return jnp.dot(sel, hi, preferred_element_type=F32) + jnp.dot(sel, lo, preferred_element_type=F32)


def _norm_matmul_kernel(x_ref, g_ref, w_ref, o_ref, u_ref):
    @pl.when(pl.program_id(1) == 0)
    def _():
        x = x_ref[...]
        ms = jnp.mean(x * x, axis=-1, keepdims=True)
        u_ref[...] = (x * lax.rsqrt(ms + NORM_EPS) * g_ref[...]).astype(BF16)

    o_ref[...] = jnp.dot(u_ref[...], w_ref[...], preferred_element_type=F32)


def _norm_matmul(x, g, w, tm, tn):
    n, d = x.shape
    m = w.shape[1]
    return pl.pallas_call(
        _norm_matmul_kernel,
        grid=(n // tm, m // tn),
        in_specs=[
            pl.BlockSpec((tm, d), lambda i, j: (i, 0)),
            pl.BlockSpec((1, d), lambda i, j: (0, 0)),
            pl.BlockSpec((d, tn), lambda i, j: (0, j)),
        ],
        out_specs=pl.BlockSpec((tm, tn), lambda i, j: (i, j)),
        out_shape=jax.ShapeDtypeStruct((n, m), F32),
        scratch_shapes=[pltpu.VMEM((tm, d), BF16)],
        compiler_params=_cparams(("parallel", "arbitrary")),
        name="in_proj",
    )(x, g, w)


def _shift_mix(x, prev_ref, mu, first):
    rows = lax.broadcasted_iota(jnp.int32, x.shape, 0)
    prev = jnp.where(first, 0.0, prev_ref[...])
    xs = jnp.where(rows == 0, prev, pltpu.roll(x, 1, 0))
    prev_ref[...] = x[x.shape[0] - 1:, :]
    return x + (xs - x) * mu


def _rwkv_prep_kernel(has_vres, *refs):
    if has_vres:
        (prkv_ref, plora_ref, vf_ref, mu1_ref, mu2_ref, w0_ref, w2_ref, a0_ref, a2_ref, g2_ref,
         kk_ref, ka_ref, rk_ref, e_ref, et_ref, ctri_ref, csel_ref, v0_ref, v1_ref, v2_ref,
         at_o, rt_o, bt_o, kt_o, vb_o, wl_o, g_o, bonus_o, prev1, prev2) = refs
    else:
        (prkv_ref, plora_ref, mu1_ref, mu2_ref, w0_ref, w2_ref, a0_ref, a2_ref, g2_ref,
         kk_ref, ka_ref, rk_ref, e_ref, et_ref, ctri_ref, csel_ref,
         at_o, rt_o, bt_o, kt_o, vb_o, wl_o, g_o, bonus_o, v_o, prev1, prev2) = refs
    first = pl.program_id(1) == 0
    p = _shift_mix(prkv_ref[...], prev1, mu1_ref[...], first)
    lo = _shift_mix(plora_ref[...], prev2, mu2_ref[...], first)
    r = p[:, :RW_WIDTH]
    k = p[:, RW_WIDTH:2 * RW_WIDTH]
    v = p[:, 2 * RW_WIDTH:]
    w_log = -_softplus(-(w0_ref[...] + _bdot(jnp.tanh(lo), w2_ref[...]))) - 0.5
    log_w = -jnp.exp(w_log)
    a = _sigmoid(a0_ref[...] + _bdot(lo, a2_ref[...]))
    g = _bdot(_sigmoid(lo), g2_ref[...])
    if has_vres:
        mix = _sigmoid(v0_ref[...] + _bdot(_bdot(v, v1_ref[...]), v2_ref[...]))
        v = v + (vf_ref[...] - v) * mix
    else:
        v_o[...] = v
    e, et = e_ref[...], et_ref[...]
    kk = k * kk_ref[...]
    inv = lax.rsqrt(_pick_dot(kk * kk, e) + 1e-12)
    kk = kk * _pick_dot(inv, et)
    k2 = k * (1.0 + (a - 1.0) * ka_ref[...])
    rk = _pick_dot(_pick_dot(r * k2 * rk_ref[...], e), et)
    cum = _pick_dot_left(ctri_ref[...], log_w)
    grow, shrink = jnp.exp(cum), jnp.exp(-cum)
    at_o[...] = (-kk * jnp.exp(cum - log_w)).astype(BF16)
    rt_o[...] = (r * grow).astype(BF16)
    bt_o[...] = (kk * a * shrink).astype(BF16)
    kt_o[...] = (k2 * shrink).astype(BF16)
    vb_o[...] = v.astype(BF16)
    wl_o[...] = jnp.exp(_pick_dot_left(csel_ref[...], cum))
    g_o[...] = g
    bonus_o[...] = rk * v


def _rwkv_prep(proj3, vfirst, prm, tt):
    b, t, _ = proj3.shape
    has_vres = vfirst is not None
    tok = lambda w, c: pl.BlockSpec((None, tt, w), lambda bi, i: (bi, i, c))
    in_specs = [tok(W_RKV, OFF_RKV // W_RKV), tok(W_LORA, OFF_LORA // W_LORA)]
    args = [proj3, proj3]
    if has_vres:
        in_specs.append(tok(RW_WIDTH, 0))
        args.append(vfirst)
    names = ["mu1", "mu2", "w0", "w2", "a0", "a2", "g2", "k_k", "k_a", "r_k", "seg", "seg_t", "chunk_tri", "chunk_last"]
    if has_vres:
        names += ["v0", "v1", "v2"]
    for nm in names:
        in_specs.append(_full(prm[nm].shape))
        args.append(prm[nm])
    out_shape = [jax.ShapeDtypeStruct((b, t, RW_WIDTH), BF16)] * 5
    out_specs = [tok(RW_WIDTH, 0)] * 5
    out_shape.append(jax.ShapeDtypeStruct((b, t // REC_CHUNK, RW_WIDTH), F32))
    out_specs.append(pl.BlockSpec((None, tt // REC_CHUNK, RW_WIDTH), lambda bi, i: (bi, i, 0)))
    out_shape += [jax.ShapeDtypeStruct((b, t, RW_WIDTH), F32)] * 2
    out_specs += [tok(RW_WIDTH, 0)] * 2
    if not has_vres:
        out_shape.append(jax.ShapeDtypeStruct((b, t, RW_WIDTH), F32))
        out_specs.append(tok(RW_WIDTH, 0))
    return pl.pallas_call(
        functools.partial(_rwkv_prep_kernel, has_vres),
        grid=(b, t // tt),
        in_specs=in_specs,
        out_specs=out_specs,
        out_shape=out_shape,
        scratch_shapes=[pltpu.VMEM((1, W_RKV), F32), pltpu.VMEM((1, W_LORA), F32)],
        compiler_params=_cparams(("parallel", "arbitrary")),
        name="rwkv_prep",
    )(*args)


def _rwkv_chunk_kernel(at_ref, rt_ref, bt_ref, kt_ref, v_ref, wl_ref, y_ref, s_ref):
    nb, tc = at_ref.shape[0], at_ref.shape[1]
    L = REC_CHUNK
    pairs = range(RW_PAIRS)

    @pl.when(pl.program_id(1) == 0)
    def _():
        s_ref[...] = jnp.zeros_like(s_ref)

    lane_l = lax.broadcasted_iota(jnp.int32, (L, LANES), 1)
    row_l = lax.broadcasted_iota(jnp.int32, (L, LANES), 0)
    is_e = lane_l < RW_HEAD
    eye = jnp.where(lane_l == row_l, 1.0, 0.0)
    row_p = lax.broadcasted_iota(jnp.int32, (4 * L, LANES), 0)
    lane_p = lax.broadcasted_iota(jnp.int32, (4 * L, LANES), 1)
    t_p, j_p = row_p % L, lane_p % L
    keep = (lane_p < 2 * L) & ((j_p < t_p) | (((row_p // L) % 2 == 1) & (j_p == t_p)))
    row_s = lax.broadcasted_iota(jnp.int32, (LANES, LANES), 0)
    lane_s = lax.broadcasted_iota(jnp.int32, (LANES, LANES), 1)
    same_head = (row_s < RW_HEAD) == (lane_s < RW_HEAD)
    zeros_b = lambda r: jnp.zeros((r, LANES), BF16)
    mm = lambda a, b: jnp.dot(a, b, preferred_element_type=F32)

    def top_block(m):
        return jnp.concatenate([m.astype(BF16), zeros_b(LANES - L)], axis=0)

    def chunk(j, carry):
        rows = pl.ds(pl.multiple_of(j * L, L), L)
        grp = pl.ds(pl.multiple_of((j // SUBLANES) * SUBLANES, SUBLANES), SUBLANES)
        pick = lax.broadcasted_iota(jnp.int32, (SUBLANES, RW_WIDTH), 0) == j % SUBLANES
        tiles = [(n, p) for n in range(nb) for p in pairs]
        wl_all = [jnp.sum(jnp.where(pick, wl_ref[n, grp, :], 0.0), axis=0, keepdims=True) for n in range(nb)]
        ld = lambda ref: [ref[n, rows, LANES * p:LANES * (p + 1)] for n, p in tiles]
        a, r, b, k, v = ld(at_ref), ld(rt_ref), ld(bt_ref), ld(kt_ref), ld(v_ref)
        ev = lambda m: jnp.where(is_e, m, jnp.zeros_like(m))
        od = lambda m: jnp.where(is_e, jnp.zeros_like(m), m)
        idx = range(len(tiles))
        sc = []
        for i in idx:
            lhs = jnp.concatenate([ev(a[i]), ev(r[i]), od(a[i]), od(r[i])], axis=0)
            rhs = jnp.concatenate([b[i], k[i], zeros_b(LANES - 2 * L)], axis=0)
            s_i = lax.dot_general(lhs, rhs, (((1,), (1,)), ((), ())), preferred_element_type=F32)
            sc.append(jnp.where(keep, s_i, 0.0))
        s0 = [s_ref[n, p] for n, p in tiles]
        x1 = [mm(jnp.concatenate([a[i], r[i]], axis=0), s0[i].astype(BF16)) for i in idx]
        below = lambda m: jnp.concatenate([zeros_b(L), m, zeros_b(LANES - 2 * L)], axis=0)
        x = [x1[i][:L] + mm(sc[i][:L].astype(BF16), below(ev(v[i])))
             + mm(sc[i][2 * L:3 * L].astype(BF16), below(od(v[i]))) for i in idx]
        nab = [jnp.where(lane_l < L, sc[i][2 * L * hh:2 * L * hh + L], 0.0) for i in idx for hh in range(2)]
        sq = lambda ms: [mm(m.astype(BF16), top_block(m)) for m in ms]
        p2 = sq(nab)
        p4 = sq(p2)
        tinv = [eye + m for m in nab]
        tinv = [t + mm(t.astype(BF16), top_block(q)) for t, q in zip(tinv, p2)]
        tinv = [t + mm(t.astype(BF16), top_block(q)) for t, q in zip(tinv, p4)]
        if L > 8:
            p8 = sq(p4)
            tinv = [t + mm(t.astype(BF16), top_block(q)) for t, q in zip(tinv, p8)]
        sab = [(mm(tinv[2 * i].astype(BF16), top_block(ev(x[i])))
                + mm(tinv[2 * i + 1].astype(BF16), top_block(od(x[i])))).astype(BF16) for i in idx]
        both = lambda m1, m2: jnp.concatenate([m1, m2, zeros_b(LANES - 2 * L)], axis=0)
        for i, (n, p) in enumerate(tiles):
            y = (x1[i][L:] + mm(sc[i][L:2 * L].astype(BF16), both(ev(sab[i]), ev(v[i])))
                 + mm(sc[i][3 * L:].astype(BF16), both(od(sab[i]), od(v[i]))))
            y_ref[n, rows, LANES * p:LANES * (p + 1)] = y
        for i, (n, p) in enumerate(tiles):
            upd = lax.dot_general(jnp.concatenate([b[i], k[i]], axis=0), jnp.concatenate([sab[i], v[i]], axis=0),
                                  (((0,), (0,)), ((), ())), preferred_element_type=F32)
            decay = jnp.broadcast_to(wl_all[n][:, LANES * p:LANES * (p + 1)], (LANES, LANES)).T
            s_ref[n, p] = decay * (s0[i] + jnp.where(same_head, upd, 0.0))
        return carry

    lax.fori_loop(0, tc // L, chunk, 0)


def _rwkv_chunk(at, rt, bt, kt, v, wl, tc):
    b, t, c = at.shape
    nb = 2 if b % 2 == 0 else 1
    tok = pl.BlockSpec((nb, tc, c), lambda bi, i: (bi, i, 0))
    return pl.pallas_call(
        _rwkv_chunk_kernel,
        grid=(b // nb, t // tc),
        in_specs=[tok, tok, tok, tok, tok, pl.BlockSpec((nb, tc // REC_CHUNK, c), lambda bi, i: (bi, i, 0))],
        out_specs=tok,
        out_shape=jax.ShapeDtypeStruct((b, t, c), F32),
        scratch_shapes=[pltpu.VMEM((nb, RW_PAIRS, LANES, LANES), F32)],
        compiler_params=_cparams(("parallel", "arbitrary")),
        name="rwkv_chunk",
    )(at, rt, bt, kt, v, wl)


def _rwkv_post_kernel(y_ref, bonus_ref, g_ref, lnw_ref, lnb_ref, e_ref, et_ref, wb_ref, o_ref):
    y = y_ref[...]
    e, et = e_ref[...], et_ref[...]
    mean = _pick_dot(_pick_dot(y, e), et) * (1.0 / RW_HEAD)
    yc = y - mean
    var = _pick_dot(_pick_dot(yc * yc, e), et) * (1.0 / RW_HEAD)
    y = yc * lax.rsqrt(var + RW_GN_EPS) * lnw_ref[...] + lnb_ref[...]
    y = (y + bonus_ref[...]) * g_ref[...]
    o_ref[...] = _bdot(y, wb_ref[...])


def _rwkv_post(y, bonus, g, prm, tt):
    b, t, c = y.shape
    tok = pl.BlockSpec((None, tt, c), lambda bi, i: (bi, i, 0))
    names = ["ln_w", "ln_b", "seg", "seg_t", "w_branch"]
    return pl.pallas_call(
        _rwkv_post_kernel,
        grid=(b, t // tt),
        in_specs=[tok, tok, tok] + [_full(prm[nm].shape) for nm in names],
        out_specs=pl.BlockSpec((None, tt, D_MODEL), lambda bi, i: (bi, i, 0)),
        out_shape=jax.ShapeDtypeStruct((b, t, D_MODEL), F32),
        compiler_params=_cparams(("parallel", "parallel")),
        name="rwkv_post",
    )(y, bonus, g, *[prm[nm] for nm in names])


def _ssd_kernel(z_ref, xbc_ref, dt_ref, cw_ref, cb_ref, dtb_ref, alog_ref, dskip_ref, ng_ref, ex_ref, wb_ref,
                o_ref, buf, st_ref):
    L = SSD_CHUNK
    first = pl.program_id(1) == 0

    @pl.when(first)
    def _():
        buf[0:8, :] = jnp.zeros((8, SSD_CONV_CH), F32)
        st_ref[...] = jnp.zeros_like(st_ref)

    buf[8:8 + L, :] = xbc_ref[...]
    conv = cb_ref[...]
    for j in range(SSD_CONV):
        off = 8 - (SSD_CONV - 1) + j
        conv = conv + buf[off:off + L, :] * cw_ref[j:j + 1, :]
    buf[0:8, :] = buf[L:L + 8, :]
    xa = conv * _sigmoid(conv)
    xs = xa[:, :SSD_WIDTH]

    dt = _softplus(dt_ref[...] + dtb_ref[...])
    a = dt * (-jnp.exp(alog_ref[...]))
    row = lax.broadcasted_iota(jnp.int32, (L, L), 0)
    col = lax.broadcasted_iota(jnp.int32, (L, L), 1)
    causal = row >= col
    acs = _pick_dot_left(causal.astype(BF16), a)
    acs_last = acs[L - 1:L, :]
    stacked = jnp.concatenate(
        [dt, jnp.exp(acs), jnp.exp(acs_last - acs), jnp.broadcast_to(jnp.exp(acs_last), (8, LANES))], axis=0)
    wide = _pick_dot(stacked, ex_ref[...])
    dt_x, ea_x, dte_x, el_x = wide[:L], wide[L:2 * L], wide[2 * L:3 * L], wide[3 * L:3 * L + 1]
    acs_t = acs.T
    xdt = xs * dt_x
    lane = lax.broadcasted_iota(jnp.int32, (L, LANES), 1)
    is_lo = lane < SSD_HEAD

    y_parts = []
    for g in range(SSD_GROUPS):
        bg = xa[:, SSD_WIDTH + SSD_STATE * g:SSD_WIDTH + SSD_STATE * (g + 1)]
        cg = xa[:, SSD_WIDTH + SSD_STATE * (SSD_GROUPS + g):SSD_WIDTH + SSD_STATE * (SSD_GROUPS + g + 1)]
        cb = lax.dot_general(cg.astype(BF16), bg.astype(BF16), (((1,), (1,)), ((), ())), preferred_element_type=F32)
        gl = slice(SSD_GW * g, SSD_GW * (g + 1))
        y_off = _bdot(cg, st_ref[g]) * ea_x[:, gl]
        diag = []
        for pp in range(SSD_HG // 2):
            acc = None
            xp = xdt[:, SSD_GW * g + LANES * pp:SSD_GW * g + LANES * (pp + 1)]
            for half in range(2):
                h = SSD_HG * g + 2 * pp + half
                seg = acs[:, h:h + 1] - acs_t[h:h + 1, :]
                lm = jnp.exp(jnp.where(causal, seg, -BIG))
                xh = jnp.where(is_lo if half == 0 else jnp.logical_not(is_lo), xp, 0.0)
                term = _bdot(lm * cb, xh)
                acc = term if acc is None else acc + term
            diag.append(acc)
        y_parts.append(jnp.concatenate(diag, axis=1) + y_off)
        st_ref[g] = st_ref[g] * el_x[:, gl] + _bdot(bg.T, xdt[:, gl] * dte_x[:, gl])
    y = jnp.concatenate(y_parts, axis=1)
    y = y + xs * dskip_ref[...]
    z = z_ref[...]
    y = y * (z * _sigmoid(z))
    normed = []
    for g in range(SSD_GROUPS):
        yg = y[:, SSD_GW * g:SSD_GW * (g + 1)]
        ms = jnp.mean(yg * yg, axis=-1, keepdims=True)
        normed.append(yg * lax.rsqrt(ms + NORM_EPS))
    y = jnp.concatenate(normed, axis=1) * ng_ref[...]
    o_ref[...] = _bdot(y, wb_ref[...])


def _ssd(proj3, prm):
    b, t, _ = proj3.shape
    L = SSD_CHUNK
    tok = lambda w, off: pl.BlockSpec((None, L, w), lambda bi, i: (bi, i, off // w))
    names = ["conv_w", "conv_b", "dt_bias", "a_log", "d_skip", "norm_g", "expand", "w_branch"]
    return pl.pallas_call(
        _ssd_kernel,
        grid=(b, t // L),
        in_specs=[tok(W_Z, OFF_Z), tok(W_XBC, OFF_XBC), tok(W_DT, OFF_DT)] + [_full(prm[nm].shape) for nm in names],
        out_specs=pl.BlockSpec((None, L, D_MODEL), lambda bi, i: (bi, i, 0)),
        out_shape=jax.ShapeDtypeStruct((b, t, D_MODEL), F32),
        scratch_shapes=[pltpu.VMEM((L + 8, SSD_CONV_CH), F32), pltpu.VMEM((SSD_GROUPS, SSD_STATE, SSD_GW), F32)],
        compiler_params=_cparams(("parallel", "arbitrary")),
        name="ssd",
    )(proj3, proj3, proj3, *[prm[nm] for nm in names])


def _mix_kernel(ya_ref, yb_ref, gate_ref, h_ref, wo_ref, gf_ref, h_o, ut_o):
    gt = gate_ref[...]
    mixed = _sigmoid(gt[:, :D_MODEL]) * ya_ref[...] + _sigmoid(gt[:, D_MODEL:]) * yb_ref[...]
    h = h_ref[...] + _bdot(mixed, wo_ref[...])
    h_o[...] = h
    ms = jnp.mean(h * h, axis=-1, keepdims=True)
    u = h * lax.rsqrt(ms + NORM_EPS) * gf_ref[...]
    ut_o[...] = u.T.astype(BF16)


def _mix(ya, yb, proj, h, prm, tm):
    n = h.shape[0]
    names = ["w_out", "norm_ffn_g"]
    tok = lambda w, c: pl.BlockSpec((tm, w), lambda i: (i, c))
    return pl.pallas_call(
        _mix_kernel,
        grid=(n // tm,),
        in_specs=[tok(D_MODEL, 0), tok(D_MODEL, 0), tok(W_GATE, OFF_GATE // W_GATE), tok(D_MODEL, 0)]
        + [_full(prm[nm].shape) for nm in names],
        out_specs=[tok(D_MODEL, 0), pl.BlockSpec((D_MODEL, tm), lambda i: (0, i))],
        out_shape=[jax.ShapeDtypeStruct((n, D_MODEL), F32), jax.ShapeDtypeStruct((D_MODEL, n), BF16)],
        compiler_params=_cparams(("parallel",)),
        name="mix",
    )(ya, yb, proj, h, *[prm[nm] for nm in names])


def _top_rows(x, n, want_rank):
    tm = x.shape[1]
    row_n = lax.broadcasted_iota(jnp.int32, (n, tm), 0)
    tops = jnp.zeros((n, tm), F32)
    rank = jnp.full(x.shape, float(n), F32) if want_rank else None
    m = None
    for k in range(n):
        m = jnp.max(x, axis=0, keepdims=True)
        tops = jnp.where(row_n == k, m, tops)
        hit = x >= m
        if want_rank:
            rank = jnp.where(hit, float(k), rank)
        if k + 1 < n:
            x = jnp.where(hit, -BIG, x)
    return tops, m, rank


def _peer_sel_kernel(ut_ref, wq_ref, keys_ref, e1_o, cnt_o, e2_o, r2_o):
    k = PEER_TOPK
    qt = jnp.dot(wq_ref[...], ut_ref[...], preferred_element_type=F32)
    for h in range(PEER_HEADS):
        s = []
        for c in range(2):
            j = 2 * h + c
            s.append(_bdot(keys_ref[j], qt[LANES * j:LANES * (j + 1), :]))
        s1, s2 = s
        top1, thr1, _ = _top_rows(s1, k, False)
        top2, _, rank2 = _top_rows(s2, k, True)
        in1 = s1 >= thr1
        in2 = rank2 < float(k)
        row8 = lax.broadcasted_iota(jnp.int32, (SUBLANES, s1.shape[1]), 0)
        groups = [top1[0:1, :] + top2]
        for k1 in range(1, SUBLANES):
            width = k // (k1 + 1)
            groups.append(jnp.where(row8 < width, top1[k1:k1 + 1, :] + top2[:SUBLANES, :], -BIG))
        groups.append(top1[SUBLANES:, :] + top2[0:1, :])
        x = jnp.concatenate(groups, axis=0)
        m0 = zsum = m = None
        for i in range(k):
            m = jnp.max(x, axis=0, keepdims=True)
            if i == 0:
                m0, zsum = m, jnp.ones_like(m)
            else:
                zsum = zsum + jnp.exp(m - m0)
            if i + 1 < k:
                x = jnp.where(x >= m, -BIG, x)
        thr = m
        cnt = jnp.zeros_like(s1)
        for k2 in range(k):
            cnt = cnt + jnp.where(s1 + top2[k2:k2 + 1, :] >= thr, 1.0, 0.0)
        e1_o[h] = jnp.where(in1, jnp.exp(s1 - top1[0:1, :]), 0.0) / zsum
        cnt_o[h] = jnp.where(in1, cnt, 0.0)
        e2_o[h] = pltpu.bitcast(jnp.where(in2, jnp.exp(s2 - top2[0:1, :]), 0.0).astype(BF16), jnp.uint32)
        r2_o[h] = pltpu.bitcast(rank2.astype(BF16), jnp.uint32)


def _peer_sel(ut, prm, tm):
    n = ut.shape[1]
    shp = (PEER_HEADS, PEER_NKEYS, n)
    ospec = pl.BlockSpec((PEER_HEADS, PEER_NKEYS, tm), lambda i: (0, 0, i))
    pshp = (PEER_HEADS, PEER_NKEYS // 2, n)
    pspec = pl.BlockSpec((PEER_HEADS, PEER_NKEYS // 2, tm), lambda i: (0, 0, i))
    return pl.pallas_call(
        _peer_sel_kernel,
        grid=(n // tm,),
        in_specs=[pl.BlockSpec((D_MODEL, tm), lambda i: (0, i)), _full(prm["wq_t"].shape), _full(prm["keys"].shape)],
        out_specs=[ospec, ospec, pspec, pspec],
        out_shape=[jax.ShapeDtypeStruct(shp, F32), jax.ShapeDtypeStruct(shp, F32),
                   jax.ShapeDtypeStruct(pshp, jnp.uint32), jax.ShapeDtypeStruct(pshp, jnp.uint32)],
        compiler_params=_cparams(("parallel",)),
        name="peer_sel",
    )(ut, prm["wq_t"], prm["keys"])


def _gated_gelu(x, gate):
    c = 0.7978845608028654
    t = jnp.tanh(x * (c + (c * 0.044715) * (x * x)))
    hx = x.astype(BF16) * 0.5
    return (hx + hx * t.astype(BF16)) * gate


def _peer_ffn_kernel(final_g, ut_ref, down_ref, upt_ref, uptp_ref, e1_ref, cnt_ref, e2_ref, r2_ref, h_ref, *rest):
    if final_g:
        gfin_ref, o_ref, *scr = rest
    else:
        o_ref, *scr = rest
    nh = len(scr) // 4
    acc, act, gbuf, hs = scr[:nh], scr[nh:2 * nh], scr[2 * nh:3 * nh], scr[3 * nh:]
    j = pl.program_id(1)
    te, th = act[0].shape
    sub = 2 * SUBLANES

    @pl.when(j == 0)
    def _():
        for a in acc:
            a[...] = jnp.zeros_like(a)
        hs[nh - 1][...] = jnp.zeros_like(hs[nh - 1])

    kb = 2
    blk = kb * PEER_NKEYS
    nblk = te // blk
    nm = D_MODEL // blk
    sbt = 4

    def project_down(hf, b):
        rows = slice(blk * b, blk * (b + 1))
        act[hf][rows, :] = jnp.dot(down_ref[rows, :], ut_ref[:, th * hf:th * (hf + 1)], preferred_element_type=F32)

    def build_gates(hf, b):
        for c in range(th // LANES):
            cl = slice(th * hf + LANES * c, th * hf + LANES * (c + 1))
            for s0 in range(0, PEER_NKEYS // sub, sbt):
                gate = [[jnp.zeros((sub, LANES), BF16) for _ in range(sbt)] for _ in range(kb)]
                for h in range(PEER_HEADS):
                    r2 = [pltpu.bitcast(r2_ref[h, SUBLANES * (s0 + s):SUBLANES * (s0 + s + 1), cl], BF16) for s in range(sbt)]
                    e2 = [pltpu.bitcast(e2_ref[h, SUBLANES * (s0 + s):SUBLANES * (s0 + s + 1), cl], BF16) for s in range(sbt)]
                    for q in range(kb):
                        ii = kb * b + q
                        cnt = jnp.broadcast_to(cnt_ref[h, ii:ii + 1, cl], (sub, LANES)).astype(BF16)
                        e1 = jnp.broadcast_to(e1_ref[h, ii:ii + 1, cl], (sub, LANES)).astype(BF16)
                        for s in range(sbt):
                            gate[q][s] = gate[q][s] + jnp.where(r2[s] < cnt, e2[s], 0.0) * e1
                for q in range(kb):
                    for s in range(sbt):
                        gw = (blk * b + PEER_NKEYS * q + sub * (s0 + s)) // 2
                        gbuf[hf][gw:gw + SUBLANES, LANES * c:LANES * (c + 1)] = pltpu.bitcast(gate[q][s], jnp.uint32)

    def activate(hf, b):
        for r0 in range(blk * b, blk * (b + 1), sub):
            for c in range(th // LANES):
                cl = slice(LANES * c, LANES * (c + 1))
                gate = pltpu.bitcast(gbuf[hf][r0 // 2:r0 // 2 + SUBLANES, cl], BF16)
                hs[hf][r0:r0 + sub, cl] = _gated_gelu(act[hf][r0:r0 + sub, cl], gate)

    def project_up(w_ref, hf, m):
        rows = slice(blk * m, blk * (m + 1))
        acc[hf][rows, :] += jnp.dot(w_ref[rows, :], hs[hf][...], preferred_element_type=F32)

    for hf in range(nh):
        for b in range(nblk):
            project_down(hf, b)
            for m in range(b * nm // nblk, (b + 1) * nm // nblk):
                if hf == 0:
                    project_up(uptp_ref, nh - 1, m)
                else:
                    project_up(upt_ref, hf - 1, m)
            build_gates(hf, b)
            if b > 0:
                activate(hf, b - 1)
        activate(hf, nblk - 1)

    @pl.when(j == pl.num_programs(1) - 1)
    def _():
        for m in range(nm):
            project_up(upt_ref, nh - 1, m)
        for hf in range(nh):
            h = h_ref[th * hf:th * (hf + 1), :] + acc[hf][...].T
            if final_g:
                ms = jnp.mean(h * h, axis=-1, keepdims=True)
                h = h * lax.rsqrt(ms + NORM_EPS) * gfin_ref[...]
            o_ref[th * hf:th * (hf + 1), :] = h


def _peer_ffn(ut, down, upt, e1, cnt, e2, r2, h, final_g, tl, te):
    n = h.shape[0]
    nh = 2 if tl % (2 * LANES) == 0 else 1
    sel = pl.BlockSpec((PEER_HEADS, PEER_NKEYS // 2, tl), lambda i, j: (0, 0, i))
    sel1 = pl.BlockSpec((PEER_HEADS, te // PEER_NKEYS, tl), lambda i, j: (0, j, i))
    in_specs = [
        pl.BlockSpec((D_MODEL, tl), lambda i, j: (0, i)),
        pl.BlockSpec((te, D_MODEL), lambda i, j: (j, 0)),
        pl.BlockSpec((D_MODEL, te), lambda i, j: (0, j)),
        pl.BlockSpec((D_MODEL, te), lambda i, j: (0, jnp.maximum(j - 1, 0))),
        sel1, sel1, sel, sel,
        pl.BlockSpec((tl, D_MODEL), lambda i, j: (i, 0)),
    ]
    args = [ut, down, upt, upt, e1, cnt, e2, r2, h]
    if final_g is not None:
        in_specs.append(_full(final_g.shape))
        args.append(final_g)
    return pl.pallas_call(
        functools.partial(_peer_ffn_kernel, final_g is not None),
        grid=(n // tl, PEER_EXPERTS // te),
        in_specs=in_specs,
        out_specs=pl.BlockSpec((tl, D_MODEL), lambda i, j: (i, 0)),
        out_shape=jax.ShapeDtypeStruct((n, D_MODEL), F32),
        scratch_shapes=([pltpu.VMEM((D_MODEL, tl // nh), F32)] * nh + [pltpu.VMEM((te, tl // nh), F32)] * nh
                        + [pltpu.VMEM((te // 2, tl // nh), jnp.uint32)] * nh + [pltpu.VMEM((te, tl // nh), BF16)] * nh),
        compiler_params=_cparams(("parallel", "arbitrary")),
        name="peer_ffn",
    )(*args)


def _pad_rows(w, start, total):
    return jnp.zeros((total, w.shape[1]), w.dtype).at[start:start + w.shape[0]].set(w)


def _pad_lanes(v, total):
    return jnp.zeros((1, total), v.dtype).at[0, :v.shape[0]].set(v)


def _row(v):
    return v.reshape(1, -1)


def _constants():
    c = jnp.arange(RW_WIDTH)[:, None] // RW_HEAD == jnp.arange(LANES)[None, :]
    seg = c.astype(BF16)
    expand = (jnp.arange(LANES)[:, None] == jnp.arange(SSD_WIDTH)[None, :] // SSD_HEAD).astype(BF16)
    return seg, seg.T, expand


def kernel(x, norm_mix_g, w_in, rwkv_mu, rwkv_w0, rwkv_w2, rwkv_a0, rwkv_a2, rwkv_g2, rwkv_v0, rwkv_v1, rwkv_v2,
           rwkv_k_k, rwkv_k_a, rwkv_r_k, rwkv_ln_w, rwkv_ln_b, w_rwkv_branch, ssd_conv_w, ssd_conv_b, ssd_dt_bias,
           ssd_A_log, ssd_D, ssd_norm_g, w_ssd_branch, w_out, norm_ffn_g, peer_w_q, peer_sub_keys, peer_down, peer_up,
           norm_final_g):
    bsz, t, d = x.shape
    n = bsz * t
    depth = w_in.shape[0]
    seg, seg_t, expand = _constants()
    h = x.reshape(n, d)
    v_first = None
    o_lora = 3 * RW_WIDTH
    o_z = o_lora + RW_LORA
    o_xbc = o_z + SSD_WIDTH
    o_dt = o_xbc + SSD_CONV_CH
    o_gate = o_dt + SSD_HEADS
    tt = min(256, t)
    step = jnp.arange(tt)
    chunk_tri = ((step[:, None] // REC_CHUNK == step[None, :] // REC_CHUNK) & (step[:, None] >= step[None, :])).astype(BF16)
    chunk_last = (jnp.arange(tt // REC_CHUNK)[:, None] * REC_CHUNK + REC_CHUNK - 1 == step[None, :]).astype(BF16)
    for l in range(depth):
        wl = w_in[l]
        zpad = lambda k: jnp.zeros((d, k), wl.dtype)
        w_cat = jnp.concatenate([
            wl[:, :o_lora], wl[:, o_xbc:o_dt], wl[:, o_z:o_xbc], wl[:, o_gate:],
            wl[:, o_lora:o_z], zpad(RW_LORA_PAD - RW_LORA), wl[:, o_dt:o_gate], zpad(LANES - SSD_HEADS)], axis=1).astype(BF16)
        proj = _norm_matmul(h, _row(norm_mix_g[l]), w_cat, tm=min(1024, n), tn=2176)
        proj3 = proj.reshape(bsz, t, PROJ_PAD)

        mu = rwkv_mu[l]
        rw = {
            "mu1": _row(mu[:o_lora]), "mu2": _pad_lanes(mu[o_lora:], RW_LORA_PAD),
            "w0": _row(rwkv_w0[l]), "w2": _pad_rows(rwkv_w2[l], 0, RW_LORA_PAD).astype(BF16),
            "a0": _row(rwkv_a0[l]), "a2": _pad_rows(rwkv_a2[l], DECAY_LORA, RW_LORA_PAD).astype(BF16),
            "g2": _pad_rows(rwkv_g2[l], DECAY_LORA + ICLR_LORA, RW_LORA_PAD).astype(BF16),
            "k_k": _row(rwkv_k_k[l]), "k_a": _row(rwkv_k_a[l]), "r_k": _row(rwkv_r_k[l]),
            "seg": seg, "seg_t": seg_t, "chunk_tri": chunk_tri, "chunk_last": chunk_last,
            "ln_w": _row(rwkv_ln_w[l]), "ln_b": _row(rwkv_ln_b[l]), "w_branch": w_rwkv_branch[l].astype(BF16),
        }
        if l > 0:
            rw["v0"] = _row(rwkv_v0[l - 1])
            rw["v1"] = jnp.zeros((RW_WIDTH, VRES_PAD), F32).at[:, :VRES_LORA].set(rwkv_v1[l - 1]).astype(BF16)
            rw["v2"] = _pad_rows(rwkv_v2[l - 1], 0, VRES_PAD).astype(BF16)
        outs = _rwkv_prep(proj3, v_first, rw, tt)
        at, rt, bt, kt, vb, w_end, g_, bonus = outs[:8]
        if l == 0:
            v_first = outs[8]
        y_r = _rwkv_chunk(at, rt, bt, kt, vb, w_end, tc=tt)
        y_a = _rwkv_post(y_r, bonus, g_, rw, tt)

        sp = {
            "conv_w": ssd_conv_w[l], "conv_b": _row(ssd_conv_b[l]),
            "dt_bias": _pad_lanes(ssd_dt_bias[l], LANES), "a_log": _pad_lanes(ssd_A_log[l], LANES),
            "d_skip": _row(jnp.repeat(ssd_D[l], SSD_HEAD)), "norm_g": _row(ssd_norm_g[l]),
            "expand": expand, "w_branch": w_ssd_branch[l].astype(BF16),
        }
        y_b = _ssd(proj3, sp)

        pp = {
            "w_out": w_out[l].astype(BF16), "norm_ffn_g": _row(norm_ffn_g[l]), "wq_t": peer_w_q[l].T.astype(BF16),
            "keys": peer_sub_keys[l].reshape(2 * PEER_HEADS, PEER_NKEYS, PEER_QDIM // 2).astype(BF16),
        }
        h, ut = _mix(y_a.reshape(n, d), y_b.reshape(n, d), proj, h, pp, tm=min(256, n))
        e1, cnt, e2, r2 = _peer_sel(ut, pp, tm=min(256, n))
        fin = _row(norm_final_g) if l == depth - 1 else None
        h = _peer_ffn(ut, peer_down[l].astype(BF16), peer_up[l].T.astype(BF16), e1, cnt, e2, r2, h, fin,
                      tl=min(512, n), te=8 * PEER_NKEYS)
    return h.reshape(bsz, t, d)
```

```python
import functools

import jax
import jax.numpy as jnp
from jax import lax
from jax.experimental import pallas as pl
from jax.experimental.pallas import tpu as pltpu

F32 = jnp.float32
BF16 = jnp.bfloat16

LANES = 128
SUBLANES = 8
NORM_EPS = 1e-6

D_MODEL = 1024
RW_HEAD = 64
RW_WIDTH = D_MODEL
RW_HEADS = RW_WIDTH // RW_HEAD
RW_PAIRS = RW_WIDTH // LANES
DECAY_LORA, ICLR_LORA, GATE_LORA, VRES_LORA = 64, 64, 160, 32
RW_LORA = DECAY_LORA + ICLR_LORA + GATE_LORA
RW_LORA_PAD = 384
VRES_PAD = 128
RW_GN_EPS = 64e-5
REC_CHUNK = 16
SSD_WIDTH = 2 * D_MODEL
SSD_HEAD = 64
SSD_HEADS = SSD_WIDTH // SSD_HEAD
SSD_STATE = 128
SSD_GROUPS = 4
SSD_GW = SSD_WIDTH // SSD_GROUPS
SSD_HG = SSD_HEADS // SSD_GROUPS
SSD_CONV = 4
SSD_CONV_CH = SSD_WIDTH + 2 * SSD_GROUPS * SSD_STATE
SSD_CHUNK = 128
PEER_HEADS = 8
PEER_NKEYS = 128
PEER_TOPK = 16
PEER_QDIM = 256
PEER_EXPERTS = PEER_NKEYS * PEER_NKEYS
BIG = 1e30

OFF_RKV, W_RKV = 0, 3 * RW_WIDTH
OFF_XBC, W_XBC = 3072, SSD_CONV_CH
OFF_Z, W_Z = 6144, SSD_WIDTH
OFF_GATE, W_GATE = 8192, 2 * D_MODEL
OFF_DT, W_DT = 10240, LANES
OFF_LORA, W_LORA = 10368, RW_LORA_PAD
PROJ_PAD = 10752

VMEM_LIMIT = 56 * 1024 * 1024


def _cparams(sem):
    return pltpu.CompilerParams(dimension_semantics=sem, vmem_limit_bytes=VMEM_LIMIT)


def _full(shape):
    n = len(shape)
    return pl.BlockSpec(shape, lambda *_: (0,) * n)


def _sigmoid(x):
    return 1.0 / (1.0 + jnp.exp(-x))


def _softplus(x):
    return jnp.maximum(x, 0.0) + jnp.log(1.0 + jnp.exp(-jnp.abs(x)))


def _bdot(a, b):
    return jnp.dot(a.astype(BF16), b.astype(BF16), preferred_element_type=F32)


def _split(x):
    hi = x.astype(BF16)
    return hi, (x - hi.astype(F32)).astype(BF16)


def _pick_dot(x, sel):
    hi, lo = _split(x)
    return jnp.dot(hi, sel, preferred_element_type=F32) + jnp.dot(lo, sel, preferred_element_type=F32)


def _pick_dot_left(sel, x):
    hi, lo = _split(x)
    return jnp.dot(sel, hi, preferred_element_type=F32) + jnp.dot(sel, lo, preferred_element_type=F32)


def _norm_matmul_kernel(x_ref, g_ref, w_ref, o_ref, u_ref):
    @pl.when(pl.program_id(1) == 0)
    def _():
        x = x_ref[...]
        ms = jnp.mean(x * x, axis=-1, keepdims=True)
        u_ref[...] = (x * lax.rsqrt(ms + NORM_EPS) * g_ref[...]).astype(BF16)

    o_ref[...] = jnp.dot(u_ref[...], w_ref[...], preferred_element_type=F32)


def _norm_matmul(x, g, w, tm, tn):
    n, d = x.shape
    m = w.shape[1]
    return pl.pallas_call(
        _norm_matmul_kernel,
        grid=(n // tm, m // tn),
        in_specs=[
            pl.BlockSpec((tm, d), lambda i, j: (i, 0)),
            pl.BlockSpec((1, d), lambda i, j: (0, 0)),
            pl.BlockSpec((d, tn), lambda i, j: (0, j)),
        ],
        out_specs=pl.BlockSpec((tm, tn), lambda i, j: (i, j)),
        out_shape=jax.ShapeDtypeStruct((n, m), F32),
        scratch_shapes=[pltpu.VMEM((tm, d), BF16)],
        compiler_params=_cparams(("parallel", "arbitrary")),
        name="in_proj",
    )(x, g, w)


def _shift_mix(x, prev_ref, mu, first):
    rows = lax.broadcasted_iota(jnp.int32, x.shape, 0)
    prev = jnp.where(first, 0.0, prev_ref[...])
    xs = jnp.where(rows == 0, prev, pltpu.roll(x, 1, 0))
    prev_ref[...] = x[x.shape[0] - 1:, :]
    return x + (xs - x) * mu


def _rwkv_prep_kernel(has_vres, *refs):
    if has_vres:
        (prkv_ref, plora_ref, vf_ref, mu1_ref, mu2_ref, w0_ref, w2_ref, a0_ref, a2_ref, g2_ref,
         kk_ref, ka_ref, rk_ref, e_ref, et_ref, ctri_ref, csel_ref, v0_ref, v1_ref, v2_ref,
         at_o, rt_o, bt_o, kt_o, vb_o, wl_o, g_o, bonus_o, prev1, prev2) = refs
    else:
        (prkv_ref, plora_ref, mu1_ref, mu2_ref, w0_ref, w2_ref, a0_ref, a2_ref, g2_ref,
         kk_ref, ka_ref, rk_ref, e_ref, et_ref, ctri_ref, csel_ref,
         at_o, rt_o, bt_o, kt_o, vb_o, wl_o, g_o, bonus_o, v_o, prev1, prev2) = refs
    first = pl.program_id(1) == 0
    p = _shift_mix(prkv_ref[...], prev1, mu1_ref[...], first)
    lo = _shift_mix(plora_ref[...], prev2, mu2_ref[...], first)
    r = p[:, :RW_WIDTH]
    k = p[:, RW_WIDTH:2 * RW_WIDTH]
    v = p[:, 2 * RW_WIDTH:]
    w_log = -_softplus(-(w0_ref[...] + _bdot(jnp.tanh(lo), w2_ref[...]))) - 0.5
    log_w = -jnp.exp(w_log)
    a = _sigmoid(a0_ref[...] + _bdot(lo, a2_ref[...]))
    g = _bdot(_sigmoid(lo), g2_ref[...])
    if has_vres:
        mix = _sigmoid(v0_ref[...] + _bdot(_bdot(v, v1_ref[...]), v2_ref[...]))
        v = v + (vf_ref[...] - v) * mix
    else:
        v_o[...] = v
    e, et = e_ref[...], et_ref[...]
    kk = k * kk_ref[...]
    inv = lax.rsqrt(_pick_dot(kk * kk, e) + 1e-12)
    kk = kk * _pick_dot(inv, et)
    k2 = k * (1.0 + (a - 1.0) * ka_ref[...])
    rk = _pick_dot(_pick_dot(r * k2 * rk_ref[...], e), et)
    cum = _pick_dot_left(ctri_ref[...], log_w)
    grow, shrink = jnp.exp(cum), jnp.exp(-cum)
    at_o[...] = (-kk * jnp.exp(cum - log_w)).astype(BF16)
    rt_o[...] = (r * grow).astype(BF16)
    bt_o[...] = (kk * a * shrink).astype(BF16)
    kt_o[...] = (k2 * shrink).astype(BF16)
    vb_o[...] = v.astype(BF16)
    wl_o[...] = jnp.exp(_pick_dot_left(csel_ref[...], cum))
    g_o[...] = g
    bonus_o[...] = rk * v


def _rwkv_prep(proj3, vfirst, prm, tt):
    b, t, _ = proj3.shape
    has_vres = vfirst is not None
    tok = lambda w, c: pl.BlockSpec((None, tt, w), lambda bi, i: (bi, i, c))
    in_specs = [tok(W_RKV, OFF_RKV // W_RKV), tok(W_LORA, OFF_LORA // W_LORA)]
    args = [proj3, proj3]
    if has_vres:
        in_specs.append(tok(RW_WIDTH, 0))
        args.append(vfirst)
    names = ["mu1", "mu2", "w0", "w2", "a0", "a2", "g2", "k_k", "k_a", "r_k", "seg", "seg_t", "chunk_tri", "chunk_last"]
    if has_vres:
        names += ["v0", "v1", "v2"]
    for nm in names:
        in_specs.append(_full(prm[nm].shape))
        args.append(prm[nm])
    out_shape = [jax.ShapeDtypeStruct((b, t, RW_WIDTH), BF16)] * 5
    out_specs = [tok(RW_WIDTH, 0)] * 5
    out_shape.append(jax.ShapeDtypeStruct((b, t // REC_CHUNK, RW_WIDTH), F32))
    out_specs.append(pl.BlockSpec((None, tt // REC_CHUNK, RW_WIDTH), lambda bi, i: (bi, i, 0)))
    out_shape += [jax.ShapeDtypeStruct((b, t, RW_WIDTH), F32)] * 2
    out_specs += [tok(RW_WIDTH, 0)] * 2
    if not has_vres:
        out_shape.append(jax.ShapeDtypeStruct((b, t, RW_WIDTH), F32))
        out_specs.append(tok(RW_WIDTH, 0))
    return pl.pallas_call(
        functools.partial(_rwkv_prep_kernel, has_vres),
        grid=(b, t // tt),
        in_specs=in_specs,
        out_specs=out_specs,
        out_shape=out_shape,
        scratch_shapes=[pltpu.VMEM((1, W_RKV), F32), pltpu.VMEM((1, W_LORA), F32)],
        compiler_params=_cparams(("parallel", "arbitrary")),
        name="rwkv_prep",
    )(*args)


def _rwkv_chunk_kernel(at_ref, rt_ref, bt_ref, kt_ref, v_ref, wl_ref, y_ref, s_ref):
    nb, tc = at_ref.shape[0], at_ref.shape[1]
    L = REC_CHUNK
    pairs = range(RW_PAIRS)

    @pl.when(pl.program_id(1) == 0)
    def _():
        s_ref[...] = jnp.zeros_like(s_ref)

    lane_l = lax.broadcasted_iota(jnp.int32, (L, LANES), 1)
    row_l = lax.broadcasted_iota(jnp.int32, (L, LANES), 0)
    is_e = lane_l < RW_HEAD
    eye = jnp.where(lane_l == row_l, 1.0, 0.0)
    row_p = lax.broadcasted_iota(jnp.int32, (4 * L, LANES), 0)
    lane_p = lax.broadcasted_iota(jnp.int32, (4 * L, LANES), 1)
    t_p, j_p = row_p % L, lane_p % L
    keep = (lane_p < 2 * L) & ((j_p < t_p) | (((row_p // L) % 2 == 1) & (j_p == t_p)))
    row_s = lax.broadcasted_iota(jnp.int32, (LANES, LANES), 0)
    lane_s = lax.broadcasted_iota(jnp.int32, (LANES, LANES), 1)
    same_head = (row_s < RW_HEAD) == (lane_s < RW_HEAD)
    zeros_b = lambda r: jnp.zeros((r, LANES), BF16)
    mm = lambda a, b: jnp.dot(a, b, preferred_element_type=F32)

    def top_block(m):
        return jnp.concatenate([m.astype(BF16), zeros_b(LANES - L)], axis=0)

    def chunk(j, carry):
        rows = pl.ds(pl.multiple_of(j * L, L), L)
        grp = pl.ds(pl.multiple_of((j // SUBLANES) * SUBLANES, SUBLANES), SUBLANES)
        pick = lax.broadcasted_iota(jnp.int32, (SUBLANES, RW_WIDTH), 0) == j % SUBLANES
        tiles = [(n, p) for n in range(nb) for p in pairs]
        wl_all = [jnp.sum(jnp.where(pick, wl_ref[n, grp, :], 0.0), axis=0, keepdims=True) for n in range(nb)]
        ld = lambda ref: [ref[n, rows, LANES * p:LANES * (p + 1)] for n, p in tiles]
        a, r, b, k, v = ld(at_ref), ld(rt_ref), ld(bt_ref), ld(kt_ref), ld(v_ref)
        ev = lambda m: jnp.where(is_e, m, jnp.zeros_like(m))
        od = lambda m: jnp.where(is_e, jnp.zeros_like(m), m)
        idx = range(len(tiles))
        sc = []
        for i in idx:
            lhs = jnp.concatenate([ev(a[i]), ev(r[i]), od(a[i]), od(r[i])], axis=0)
            rhs = jnp.concatenate([b[i], k[i], zeros_b(LANES - 2 * L)], axis=0)
            s_i = lax.dot_general(lhs, rhs, (((1,), (1,)), ((), ())), preferred_element_type=F32)
            sc.append(jnp.where(keep, s_i, 0.0))
        s0 = [s_ref[n, p] for n, p in tiles]
        x1 = [mm(jnp.concatenate([a[i], r[i]], axis=0), s0[i].astype(BF16)) for i in idx]
        below = lambda m: jnp.concatenate([zeros_b(L), m, zeros_b(LANES - 2 * L)], axis=0)
        x = [x1[i][:L] + mm(sc[i][:L].astype(BF16), below(ev(v[i])))
             + mm(sc[i][2 * L:3 * L].astype(BF16), below(od(v[i]))) for i in idx]
        nab = [jnp.where(lane_l < L, sc[i][2 * L * hh:2 * L * hh + L], 0.0) for i in idx for hh in range(2)]
        sq = lambda ms: [mm(m.astype(BF16), top_block(m)) for m in ms]
        p2 = sq(nab)
        p4 = sq(p2)
        tinv = [eye + m for m in nab]
        tinv = [t + mm(t.astype(BF16), top_block(q)) for t, q in zip(tinv, p2)]
        tinv = [t + mm(t.astype(BF16), top_block(q)) for t, q in zip(tinv, p4)]
        if L > 8:
            p8 = sq(p4)
            tinv = [t + mm(t.astype(BF16), top_block(q)) for t, q in zip(tinv, p8)]
        sab = [(mm(tinv[2 * i].astype(BF16), top_block(ev(x[i])))
                + mm(tinv[2 * i + 1].astype(BF16), top_block(od(x[i])))).astype(BF16) for i in idx]
        both = lambda m1, m2: jnp.concatenate([m1, m2, zeros_b(LANES - 2 * L)], axis=0)
        for i, (n, p) in enumerate(tiles):
            y = (x1[i][L:] + mm(sc[i][L:2 * L].astype(BF16), both(ev(sab[i]), ev(v[i])))
                 + mm(sc[i][3 * L:].astype(BF16), both(od(sab[i]), od(v[i]))))
            y_ref[n, rows, LANES * p:LANES * (p + 1)] = y
        for i, (n, p) in enumerate(tiles):
            upd = lax.dot_general(jnp.concatenate([b[i], k[i]], axis=0), jnp.concatenate([sab[i], v[i]], axis=0),
                                  (((0,), (0,)), ((), ())), preferred_element_type=F32)
            decay = jnp.broadcast_to(wl_all[n][:, LANES * p:LANES * (p + 1)], (LANES, LANES)).T
            s_ref[n, p] = decay * (s0[i] + jnp.where(same_head, upd, 0.0))
        return carry

    lax.fori_loop(0, tc // L, chunk, 0)


def _rwkv_chunk(at, rt, bt, kt, v, wl, tc):
    b, t, c = at.shape
    nb = 2 if b % 2 == 0 else 1
    tok = pl.BlockSpec((nb, tc, c), lambda bi, i: (bi, i, 0))
    return pl.pallas_call(
        _rwkv_chunk_kernel,
        grid=(b // nb, t // tc),
        in_specs=[tok, tok, tok, tok, tok, pl.BlockSpec((nb, tc // REC_CHUNK, c), lambda bi, i: (bi, i, 0))],
        out_specs=tok,
        out_shape=jax.ShapeDtypeStruct((b, t, c), F32),
        scratch_shapes=[pltpu.VMEM((nb, RW_PAIRS, LANES, LANES), F32)],
        compiler_params=_cparams(("parallel", "arbitrary")),
        name="rwkv_chunk",
    )(at, rt, bt, kt, v, wl)


def _rwkv_post_kernel(y_ref, bonus_ref, g_ref, lnw_ref, lnb_ref, e_ref, et_ref, wb_ref, o_ref):
    y = y_ref[...]
    e, et = e_ref[...], et_ref[...]
    mean = _pick_dot(_pick_dot(y, e), et) * (1.0 / RW_HEAD)
    yc = y - mean
    var = _pick_dot(_pick_dot(yc * yc, e), et) * (1.0 / RW_HEAD)
    y = yc * lax.rsqrt(var + RW_GN_EPS) * lnw_ref[...] + lnb_ref[...]
    y = (y + bonus_ref[...]) * g_ref[...]
    o_ref[...] = _bdot(y, wb_ref[...])


def _rwkv_post(y, bonus, g, prm, tt):
    b, t, c = y.shape
    tok = pl.BlockSpec((None, tt, c), lambda bi, i: (bi, i, 0))
    names = ["ln_w", "ln_b", "seg", "seg_t", "w_branch"]
    return pl.pallas_call(
        _rwkv_post_kernel,
        grid=(b, t // tt),
        in_specs=[tok, tok, tok] + [_full(prm[nm].shape) for nm in names],
        out_specs=pl.BlockSpec((None, tt, D_MODEL), lambda bi, i: (bi, i, 0)),
        out_shape=jax.ShapeDtypeStruct((b, t, D_MODEL), F32),
        compiler_params=_cparams(("parallel", "parallel")),
        name="rwkv_post",
    )(y, bonus, g, *[prm[nm] for nm in names])


def _ssd_kernel(z_ref, xbc_ref, dt_ref, cw_ref, cb_ref, dtb_ref, alog_ref, dskip_ref, ng_ref, ex_ref, tri_ref, wb_ref,
                o_ref, buf, st_ref):
    L = SSD_CHUNK
    first = pl.program_id(1) == 0

    @pl.when(first)
    def _():
        buf[0:8, :] = jnp.zeros((8, SSD_CONV_CH), F32)
        st_ref[...] = jnp.zeros_like(st_ref)

    buf[8:8 + L, :] = xbc_ref[...]
    window = buf[...]
    conv = cb_ref[...] + window[8:, :] * cw_ref[SSD_CONV - 1:SSD_CONV, :]
    for j in range(SSD_CONV - 1):
        lag = SSD_CONV - 1 - j
        conv = conv + pltpu.roll(window, lag, 0)[8:, :] * cw_ref[j:j + 1, :]
    buf[0:8, :] = window[L:, :]
    xa = conv * _sigmoid(conv)
    xs = xa[:, :SSD_WIDTH]

    dt = _softplus(dt_ref[...] + dtb_ref[...])
    a = dt * (-jnp.exp(alog_ref[...]))
    row = lax.broadcasted_iota(jnp.int32, (L, L), 0)
    col = lax.broadcasted_iota(jnp.int32, (L, L), 1)
    causal = row >= col
    acs = _pick_dot_left(tri_ref[...], a)
    acs_last = acs[L - 1:L, :]
    stacked = jnp.concatenate(
        [dt, jnp.exp(acs), jnp.exp(acs_last - acs), jnp.broadcast_to(jnp.exp(acs_last), (8, LANES))], axis=0)
    wide = _pick_dot(stacked, ex_ref[...])
    dt_x, ea_x, dte_x, el_x = wide[:L], wide[L:2 * L], wide[2 * L:3 * L], wide[3 * L:3 * L + 1]
    acs_t = acs.T
    xdt = xs * dt_x
    lane = lax.broadcasted_iota(jnp.int32, (L, LANES), 1)
    is_lo = lane < SSD_HEAD

    y_parts = []
    for g in range(SSD_GROUPS):
        bg = xa[:, SSD_WIDTH + SSD_STATE * g:SSD_WIDTH + SSD_STATE * (g + 1)]
        cg = xa[:, SSD_WIDTH + SSD_STATE * (SSD_GROUPS + g):SSD_WIDTH + SSD_STATE * (SSD_GROUPS + g + 1)]
        cb = lax.dot_general(cg.astype(BF16), bg.astype(BF16), (((1,), (1,)), ((), ())), preferred_element_type=F32)
        gl = slice(SSD_GW * g, SSD_GW * (g + 1))
        y_off = _bdot(cg, st_ref[g]) * ea_x[:, gl]
        diag = []
        for pp in range(SSD_HG // 2):
            acc = None
            xp = xdt[:, SSD_GW * g + LANES * pp:SSD_GW * g + LANES * (pp + 1)]
            for half in range(2):
                h = SSD_HG * g + 2 * pp + half
                seg = acs[:, h:h + 1] - acs_t[h:h + 1, :]
                lm = jnp.exp(jnp.where(causal, seg, -BIG))
                xh = jnp.where(is_lo if half == 0 else jnp.logical_not(is_lo), xp, 0.0)
                term = _bdot(lm * cb, xh)
                acc = term if acc is None else acc + term
            diag.append(acc)
        y_parts.append(jnp.concatenate(diag, axis=1) + y_off)
        st_ref[g] = st_ref[g] * el_x[:, gl] + _bdot(bg.T, xdt[:, gl] * dte_x[:, gl])
    y = jnp.concatenate(y_parts, axis=1)
    y = y + xs * dskip_ref[...]
    z = z_ref[...]
    y = y * (z * _sigmoid(z))
    normed = []
    for g in range(SSD_GROUPS):
        yg = y[:, SSD_GW * g:SSD_GW * (g + 1)]
        ms = jnp.mean(yg * yg, axis=-1, keepdims=True)
        normed.append(yg * lax.rsqrt(ms + NORM_EPS))
    y = jnp.concatenate(normed, axis=1) * ng_ref[...]
    o_ref[...] = _bdot(y, wb_ref[...])


def _ssd(proj3, prm):
    b, t, _ = proj3.shape
    L = SSD_CHUNK
    tok = lambda w, off: pl.BlockSpec((None, L, w), lambda bi, i: (bi, i, off // w))
    names = ["conv_w", "conv_b", "dt_bias", "a_log", "d_skip", "norm_g", "expand", "tri", "w_branch"]
    return pl.pallas_call(
        _ssd_kernel,
        grid=(b, t // L),
        in_specs=[tok(W_Z, OFF_Z), tok(W_XBC, OFF_XBC), tok(W_DT, OFF_DT)] + [_full(prm[nm].shape) for nm in names],
        out_specs=pl.BlockSpec((None, L, D_MODEL), lambda bi, i: (bi, i, 0)),
        out_shape=jax.ShapeDtypeStruct((b, t, D_MODEL), F32),
        scratch_shapes=[pltpu.VMEM((L + 8, SSD_CONV_CH), F32), pltpu.VMEM((SSD_GROUPS, SSD_STATE, SSD_GW), F32)],
        compiler_params=_cparams(("parallel", "arbitrary")),
        name="ssd",
    )(proj3, proj3, proj3, *[prm[nm] for nm in names])


def _mix_kernel(ya_ref, yb_ref, gate_ref, h_ref, wo_ref, gf_ref, h_o, ut_o):
    gt = gate_ref[...]
    mixed = _sigmoid(gt[:, :D_MODEL]) * ya_ref[...] + _sigmoid(gt[:, D_MODEL:]) * yb_ref[...]
    h = h_ref[...] + _bdot(mixed, wo_ref[...])
    h_o[...] = h
    ms = jnp.mean(h * h, axis=-1, keepdims=True)
    u = h * lax.rsqrt(ms + NORM_EPS) * gf_ref[...]
    ut_o[...] = pltpu.bitcast(u.T.astype(BF16), jnp.uint32)


def _mix(ya, yb, proj, h, prm, tm):
    n = h.shape[0]
    names = ["w_out", "norm_ffn_g"]
    tok = lambda w, c: pl.BlockSpec((tm, w), lambda i: (i, c))
    return pl.pallas_call(
        _mix_kernel,
        grid=(n // tm,),
        in_specs=[tok(D_MODEL, 0), tok(D_MODEL, 0), tok(W_GATE, OFF_GATE // W_GATE), tok(D_MODEL, 0)]
        + [_full(prm[nm].shape) for nm in names],
        out_specs=[tok(D_MODEL, 0), pl.BlockSpec((D_MODEL // 2, tm), lambda i: (0, i))],
        out_shape=[jax.ShapeDtypeStruct((n, D_MODEL), F32), jax.ShapeDtypeStruct((D_MODEL // 2, n), jnp.uint32)],
        compiler_params=_cparams(("parallel",)),
        name="mix",
    )(ya, yb, proj, h, *[prm[nm] for nm in names])


def _top_rows(x, n, want_rank):
    tm = x.shape[1]
    row_n = lax.broadcasted_iota(jnp.int32, (n, tm), 0)
    tops = jnp.zeros((n, tm), F32)
    rank = jnp.full(x.shape, float(n), F32) if want_rank else None
    m = None
    for k in range(n):
        m = jnp.max(x, axis=0, keepdims=True)
        tops = jnp.where(row_n == k, m, tops)
        hit = x >= m
        if want_rank:
            rank = jnp.where(hit, float(k), rank)
        if k + 1 < n:
            x = jnp.where(hit, -BIG, x)
    return tops, m, rank


def _peer_sel_kernel(ut_ref, wq_ref, keys_ref, e1_o, cnt_o, e2_o, r2_o):
    k = PEER_TOPK
    qt = jnp.dot(wq_ref[...], pltpu.bitcast(ut_ref[...], BF16), preferred_element_type=F32)
    for h in range(PEER_HEADS):
        s = []
        for c in range(2):
            j = 2 * h + c
            s.append(_bdot(keys_ref[j], qt[LANES * j:LANES * (j + 1), :]))
        s1, s2 = s
        top1, thr1, _ = _top_rows(s1, k, False)
        top2, _, rank2 = _top_rows(s2, k, True)
        in1 = s1 >= thr1
        in2 = rank2 < float(k)
        row8 = lax.broadcasted_iota(jnp.int32, (SUBLANES, s1.shape[1]), 0)
        groups = [top1[0:1, :] + top2]
        for k1 in range(1, SUBLANES):
            width = k // (k1 + 1)
            groups.append(jnp.where(row8 < width, top1[k1:k1 + 1, :] + top2[:SUBLANES, :], -BIG))
        groups.append(top1[SUBLANES:, :] + top2[0:1, :])
        x = jnp.concatenate(groups, axis=0)
        m0 = zsum = m = None
        for i in range(k):
            m = jnp.max(x, axis=0, keepdims=True)
            if i == 0:
                m0, zsum = m, jnp.ones_like(m)
            else:
                zsum = zsum + jnp.exp(m - m0)
            if i + 1 < k:
                x = jnp.where(x >= m, -BIG, x)
        thr = m
        cnt_rank = jnp.zeros_like(top1)
        for k2 in range(k):
            cnt_rank = cnt_rank + jnp.where(top1 + top2[k2:k2 + 1, :] >= thr, 1.0, 0.0)
        cnt = jnp.zeros_like(s1)
        for k1 in range(k):
            cnt = jnp.where(s1 == top1[k1:k1 + 1, :], cnt_rank[k1:k1 + 1, :], cnt)
        e1_o[h] = jnp.where(in1, jnp.exp(s1 - top1[0:1, :]), 0.0) / zsum
        cnt_o[h] = cnt
        e2_o[h] = pltpu.bitcast(jnp.where(in2, jnp.exp(s2 - top2[0:1, :]), 0.0).astype(BF16), jnp.uint32)
        r2_o[h] = pltpu.bitcast(rank2.astype(BF16), jnp.uint32)


def _peer_sel(ut, prm, tm):
    n = ut.shape[1]
    shp = (PEER_HEADS, PEER_NKEYS, n)
    ospec = pl.BlockSpec((PEER_HEADS, PEER_NKEYS, tm), lambda i: (0, 0, i))
    pshp = (PEER_HEADS, PEER_NKEYS // 2, n)
    pspec = pl.BlockSpec((PEER_HEADS, PEER_NKEYS // 2, tm), lambda i: (0, 0, i))
    return pl.pallas_call(
        _peer_sel_kernel,
        grid=(n // tm,),
        in_specs=[pl.BlockSpec((D_MODEL // 2, tm), lambda i: (0, i)), _full(prm["wq_t"].shape), _full(prm["keys"].shape)],
        out_specs=[ospec, ospec, pspec, pspec],
        out_shape=[jax.ShapeDtypeStruct(shp, F32), jax.ShapeDtypeStruct(shp, F32),
                   jax.ShapeDtypeStruct(pshp, jnp.uint32), jax.ShapeDtypeStruct(pshp, jnp.uint32)],
        compiler_params=_cparams(("parallel",)),
        name="peer_sel",
    )(ut, prm["wq_t"], prm["keys"])


def _gated_gelu(x, gate):
    c = 0.7978845608028654
    t = jnp.tanh(x * (c + (c * 0.044715) * (x * x)))
    hx = x.astype(BF16) * 0.5
    return (hx + hx * t.astype(BF16)) * gate


def _peer_ffn_kernel(final_g, ut_ref, down_ref, upt_ref, uptp_ref, e1_ref, cnt_ref, e2_ref, r2_ref, h_ref, *rest):
    if final_g:
        gfin_ref, o_ref, *scr = rest
    else:
        o_ref, *scr = rest
    nh = len(scr) // 5
    acc, act, gbuf, hs, tok = (scr[nh * q:nh * (q + 1)] for q in range(5))
    j = pl.program_id(1)
    te, th = act[0].shape
    sub = 2 * SUBLANES

    @pl.when(j == 0)
    def _():
        for hf in range(nh):
            acc[hf][...] = jnp.zeros_like(acc[hf])
            tok[hf][...] = pltpu.bitcast(ut_ref[:, th * hf:th * (hf + 1)], BF16)
        hs[nh - 1][...] = jnp.zeros_like(hs[nh - 1])

    kb = 2
    blk = kb * PEER_NKEYS
    nblk = te // blk
    nm = D_MODEL // blk
    sbt = 4

    def project_down(hf, b):
        rows = slice(blk * b, blk * (b + 1))
        lhs = pltpu.bitcast(down_ref[blk // 2 * b:blk // 2 * (b + 1), :], BF16)
        act[hf][rows, :] = jnp.dot(lhs, tok[hf][...], preferred_element_type=F32)

    def build_gates(hf, b):
        for c in range(th // LANES):
            cl = slice(th * hf + LANES * c, th * hf + LANES * (c + 1))
            first = lambda ref, h, q: jnp.broadcast_to(ref[h, kb * b + q:kb * b + q + 1, cl], (sub, LANES)).astype(BF16)
            cnt = [[first(cnt_ref, h, q) for q in range(kb)] for h in range(PEER_HEADS)]
            e1 = [[first(e1_ref, h, q) for q in range(kb)] for h in range(PEER_HEADS)]
            for s0 in range(0, PEER_NKEYS // sub, sbt):
                gate = [[jnp.zeros((sub, LANES), BF16) for _ in range(sbt)] for _ in range(kb)]
                for h in range(PEER_HEADS):
                    r2 = [pltpu.bitcast(r2_ref[h, SUBLANES * (s0 + s):SUBLANES * (s0 + s + 1), cl], BF16) for s in range(sbt)]
                    e2 = [pltpu.bitcast(e2_ref[h, SUBLANES * (s0 + s):SUBLANES * (s0 + s + 1), cl], BF16) for s in range(sbt)]
                    for q in range(kb):
                        for s in range(sbt):
                            gate[q][s] = gate[q][s] + jnp.where(r2[s] < cnt[h][q], e2[s], 0.0) * e1[h][q]
                for q in range(kb):
                    for s in range(sbt):
                        gw = (blk * b + PEER_NKEYS * q + sub * (s0 + s)) // 2
                        gbuf[hf][gw:gw + SUBLANES, LANES * c:LANES * (c + 1)] = pltpu.bitcast(gate[q][s], jnp.uint32)

    def activate(hf, b):
        for r0 in range(blk * b, blk * (b + 1), sub):
            for c in range(th // LANES):
                cl = slice(LANES * c, LANES * (c + 1))
                gate = pltpu.bitcast(gbuf[hf][r0 // 2:r0 // 2 + SUBLANES, cl], BF16)
                hs[hf][r0:r0 + sub, cl] = _gated_gelu(act[hf][r0:r0 + sub, cl], gate)

    def project_up(w_ref, hf, m):
        rows = slice(blk * m, blk * (m + 1))
        lhs = pltpu.bitcast(w_ref[blk // 2 * m:blk // 2 * (m + 1), :], BF16)
        acc[hf][rows, :] += jnp.dot(lhs, hs[hf][...], preferred_element_type=F32)

    for hf in range(nh):
        for b in range(nblk):
            project_down(hf, b)
            for m in range(b * nm // nblk, (b + 1) * nm // nblk):
                if hf == 0:
                    project_up(uptp_ref, nh - 1, m)
                else:
                    project_up(upt_ref, hf - 1, m)
            build_gates(hf, b)
            if b > 0:
                activate(hf, b - 1)
        activate(hf, nblk - 1)

    @pl.when(j == pl.num_programs(1) - 1)
    def _():
        for m in range(nm):
            project_up(upt_ref, nh - 1, m)
        for hf in range(nh):
            h = h_ref[th * hf:th * (hf + 1), :] + acc[hf][...].T
            if final_g:
                ms = jnp.mean(h * h, axis=-1, keepdims=True)
                h = h * lax.rsqrt(ms + NORM_EPS) * gfin_ref[...]
            o_ref[th * hf:th * (hf + 1), :] = h


def _peer_ffn(ut, down, upt, e1, cnt, e2, r2, h, final_g, tl, te):
    n = h.shape[0]
    nh = 2 if tl % (2 * LANES) == 0 else 1
    sel = pl.BlockSpec((PEER_HEADS, PEER_NKEYS // 2, tl), lambda i, j: (0, 0, i))
    sel1 = pl.BlockSpec((PEER_HEADS, te // PEER_NKEYS, tl), lambda i, j: (0, j, i))
    in_specs = [
        pl.BlockSpec((D_MODEL // 2, tl), lambda i, j: (0, i)),
        pl.BlockSpec((te // 2, D_MODEL), lambda i, j: (j, 0)),
        pl.BlockSpec((D_MODEL // 2, te), lambda i, j: (0, j)),
        pl.BlockSpec((D_MODEL // 2, te), lambda i, j: (0, jnp.maximum(j - 1, 0))),
        sel1, sel1, sel, sel,
        pl.BlockSpec((tl, D_MODEL), lambda i, j: (i, 0)),
    ]
    args = [ut, down, upt, upt, e1, cnt, e2, r2, h]
    if final_g is not None:
        in_specs.append(_full(final_g.shape))
        args.append(final_g)
    return pl.pallas_call(
        functools.partial(_peer_ffn_kernel, final_g is not None),
        grid=(n // tl, PEER_EXPERTS // te),
        in_specs=in_specs,
        out_specs=pl.BlockSpec((tl, D_MODEL), lambda i, j: (i, 0)),
        out_shape=jax.ShapeDtypeStruct((n, D_MODEL), F32),
        scratch_shapes=([pltpu.VMEM((D_MODEL, tl // nh), F32)] * nh + [pltpu.VMEM((te, tl // nh), F32)] * nh
                        + [pltpu.VMEM((te // 2, tl // nh), jnp.uint32)] * nh + [pltpu.VMEM((te, tl // nh), BF16)] * nh
                        + [pltpu.VMEM((D_MODEL, tl // nh), BF16)] * nh),
        compiler_params=_cparams(("parallel", "arbitrary")),
        name="peer_ffn",
    )(*args)


def _pack_weight_kernel(transpose, w_ref, o_ref):
    w = w_ref[...]
    if transpose:
        w = w.T
    o_ref[...] = pltpu.bitcast(w.astype(BF16), jnp.uint32)


def _pack_weight(w, transpose):
    r, c = w.shape
    t = D_MODEL
    out_shape, out_map = ((c // 2, r), lambda i, j: (j, i)) if transpose else ((r // 2, c), lambda i, j: (i, j))
    return pl.pallas_call(
        functools.partial(_pack_weight_kernel, transpose),
        grid=(r // t, c // t),
        in_specs=[pl.BlockSpec((t, t), lambda i, j: (i, j))],
        out_specs=pl.BlockSpec((t // 2, t), out_map),
        out_shape=jax.ShapeDtypeStruct(out_shape, jnp.uint32),
        compiler_params=_cparams(("parallel", "parallel")),
        name="pack_weight",
    )(w)


def _pad_rows(w, start, total):
    return jnp.zeros((total, w.shape[1]), w.dtype).at[start:start + w.shape[0]].set(w)


def _pad_lanes(v, total):
    return jnp.zeros((1, total), v.dtype).at[0, :v.shape[0]].set(v)


def _row(v):
    return v.reshape(1, -1)


def _constants():
    c = jnp.arange(RW_WIDTH)[:, None] // RW_HEAD == jnp.arange(LANES)[None, :]
    seg = c.astype(BF16)
    expand = (jnp.arange(LANES)[:, None] == jnp.arange(SSD_WIDTH)[None, :] // SSD_HEAD).astype(BF16)
    return seg, seg.T, expand


def kernel(x, norm_mix_g, w_in, rwkv_mu, rwkv_w0, rwkv_w2, rwkv_a0, rwkv_a2, rwkv_g2, rwkv_v0, rwkv_v1, rwkv_v2,
           rwkv_k_k, rwkv_k_a, rwkv_r_k, rwkv_ln_w, rwkv_ln_b, w_rwkv_branch, ssd_conv_w, ssd_conv_b, ssd_dt_bias,
           ssd_A_log, ssd_D, ssd_norm_g, w_ssd_branch, w_out, norm_ffn_g, peer_w_q, peer_sub_keys, peer_down, peer_up,
           norm_final_g):
    bsz, t, d = x.shape
    n = bsz * t
    depth = w_in.shape[0]
    seg, seg_t, expand = _constants()
    h = x.reshape(n, d)
    v_first = None
    o_lora = 3 * RW_WIDTH
    o_z = o_lora + RW_LORA
    o_xbc = o_z + SSD_WIDTH
    o_dt = o_xbc + SSD_CONV_CH
    o_gate = o_dt + SSD_HEADS
    tt = min(256, t)
    step = jnp.arange(tt)
    chunk_tri = ((step[:, None] // REC_CHUNK == step[None, :] // REC_CHUNK) & (step[:, None] >= step[None, :])).astype(BF16)
    chunk_last = (jnp.arange(tt // REC_CHUNK)[:, None] * REC_CHUNK + REC_CHUNK - 1 == step[None, :]).astype(BF16)
    ssd_tri = (jnp.arange(SSD_CHUNK)[:, None] >= jnp.arange(SSD_CHUNK)[None, :]).astype(BF16)
    for l in range(depth):
        wl = w_in[l]
        zpad = lambda k: jnp.zeros((d, k), wl.dtype)
        w_cat = jnp.concatenate([
            wl[:, :o_lora], wl[:, o_xbc:o_dt], wl[:, o_z:o_xbc], wl[:, o_gate:],
            wl[:, o_dt:o_gate], zpad(LANES - SSD_HEADS), wl[:, o_lora:o_z], zpad(RW_LORA_PAD - RW_LORA)], axis=1).astype(BF16)
        proj = _norm_matmul(h, _row(norm_mix_g[l]), w_cat, tm=min(1024, n), tn=1792)
        proj3 = proj.reshape(bsz, t, PROJ_PAD)

        mu = rwkv_mu[l]
        rw = {
            "mu1": _row(mu[:o_lora]), "mu2": _pad_lanes(mu[o_lora:], RW_LORA_PAD),
            "w0": _row(rwkv_w0[l]), "w2": _pad_rows(rwkv_w2[l], 0, RW_LORA_PAD).astype(BF16),
            "a0": _row(rwkv_a0[l]), "a2": _pad_rows(rwkv_a2[l], DECAY_LORA, RW_LORA_PAD).astype(BF16),
            "g2": _pad_rows(rwkv_g2[l], DECAY_LORA + ICLR_LORA, RW_LORA_PAD).astype(BF16),
            "k_k": _row(rwkv_k_k[l]), "k_a": _row(rwkv_k_a[l]), "r_k": _row(rwkv_r_k[l]),
            "seg": seg, "seg_t": seg_t, "chunk_tri": chunk_tri, "chunk_last": chunk_last,
            "ln_w": _row(rwkv_ln_w[l]), "ln_b": _row(rwkv_ln_b[l]), "w_branch": w_rwkv_branch[l].astype(BF16),
        }
        if l > 0:
            rw["v0"] = _row(rwkv_v0[l - 1])
            rw["v1"] = jnp.zeros((RW_WIDTH, VRES_PAD), F32).at[:, :VRES_LORA].set(rwkv_v1[l - 1]).astype(BF16)
            rw["v2"] = _pad_rows(rwkv_v2[l - 1], 0, VRES_PAD).astype(BF16)
        outs = _rwkv_prep(proj3, v_first, rw, tt)
        at, rt, bt, kt, vb, w_end, g_, bonus = outs[:8]
        if l == 0:
            v_first = outs[8]
        y_r = _rwkv_chunk(at, rt, bt, kt, vb, w_end, tc=tt)
        y_a = _rwkv_post(y_r, bonus, g_, rw, tt)

        sp = {
            "conv_w": ssd_conv_w[l], "conv_b": _row(ssd_conv_b[l]),
            "dt_bias": _pad_lanes(ssd_dt_bias[l], LANES), "a_log": _pad_lanes(ssd_A_log[l], LANES),
            "d_skip": _row(jnp.repeat(ssd_D[l], SSD_HEAD)), "norm_g": _row(ssd_norm_g[l]),
            "expand": expand, "tri": ssd_tri, "w_branch": w_ssd_branch[l].astype(BF16),
        }
        y_b = _ssd(proj3, sp)

        pp = {
            "w_out": w_out[l].astype(BF16), "norm_ffn_g": _row(norm_ffn_g[l]), "wq_t": peer_w_q[l].T.astype(BF16),
            "keys": peer_sub_keys[l].reshape(2 * PEER_HEADS, PEER_NKEYS, PEER_QDIM // 2).astype(BF16),
        }
        h, ut = _mix(y_a.reshape(n, d), y_b.reshape(n, d), proj, h, pp, tm=min(256, n))
        e1, cnt, e2, r2 = _peer_sel(ut, pp, tm=min(256, n))
        fin = _row(norm_final_g) if l == depth - 1 else None
        h = _peer_ffn(ut, _pack_weight(peer_down[l], False), _pack_weight(peer_up[l], True), e1, cnt, e2, r2, h, fin,
                      tl=min(512, n), te=8 * PEER_NKEYS)
    return h.reshape(bsz, t, d)
```

```python
import functools

import jax
import jax.numpy as jnp
from jax import lax
from jax.experimental import pallas as pl
from jax.experimental.pallas import tpu as pltpu

F32 = jnp.float32
BF16 = jnp.bfloat16

LANES = 128
SUBLANES = 8
NORM_EPS = 1e-6

D_MODEL = 1024
RW_HEAD = 64
RW_WIDTH = D_MODEL
RW_HEADS = RW_WIDTH // RW_HEAD
RW_PAIRS = RW_WIDTH // LANES
DECAY_LORA, ICLR_LORA, GATE_LORA, VRES_LORA = 64, 64, 160, 32
RW_LORA = DECAY_LORA + ICLR_LORA + GATE_LORA
RW_LORA_PAD = 384
VRES_PAD = 128
RW_GN_EPS = 64e-5
REC_CHUNK = 16
SSD_WIDTH = 2 * D_MODEL
SSD_HEAD = 64
SSD_HEADS = SSD_WIDTH // SSD_HEAD
SSD_STATE = 128
SSD_GROUPS = 4
SSD_GW = SSD_WIDTH // SSD_GROUPS
SSD_HG = SSD_HEADS // SSD_GROUPS
SSD_CONV = 4
SSD_CONV_CH = SSD_WIDTH + 2 * SSD_GROUPS * SSD_STATE
SSD_CHUNK = 128
PEER_HEADS = 8
PEER_NKEYS = 128
PEER_TOPK = 16
PEER_QDIM = 256
PEER_EXPERTS = PEER_NKEYS * PEER_NKEYS
BIG = 1e30

OFF_RKV, W_RKV = 0, 3 * RW_WIDTH
OFF_XBC, W_XBC = 3072, SSD_CONV_CH
OFF_Z, W_Z = 6144, SSD_WIDTH
OFF_GATE, W_GATE = 8192, 2 * D_MODEL
OFF_DT, W_DT = 10240, LANES
OFF_LORA, W_LORA = 10368, RW_LORA_PAD
PROJ_PAD = 10752

VMEM_LIMIT = 56 * 1024 * 1024
TOK_TILE = 256
PROJ_TILE = (1024, 1792)
FFN_TILE = (512, 8 * PEER_NKEYS)


def _cparams(sem):
    return pltpu.CompilerParams(dimension_semantics=sem, vmem_limit_bytes=VMEM_LIMIT)


def _full(shape):
    n = len(shape)
    return pl.BlockSpec(shape, lambda *_: (0,) * n)


def _sigmoid(x):
    return 1.0 / (1.0 + jnp.exp(-x))


def _softplus(x):
    return jnp.maximum(x, 0.0) + jnp.log(1.0 + jnp.exp(-jnp.abs(x)))


def _bdot(a, b):
    return jnp.dot(a.astype(BF16), b.astype(BF16), preferred_element_type=F32)


def _split(x):
    hi = x.astype(BF16)
    return hi, (x - hi.astype(F32)).astype(BF16)


def _pick_dot(x, sel):
    hi, lo = _split(x)
    return jnp.dot(hi, sel, preferred_element_type=F32) + jnp.dot(lo, sel, preferred_element_type=F32)


def _pick_dot_left(sel, x):
    hi, lo = _split(x)
    return jnp.dot(sel, hi, preferred_element_type=F32) + jnp.dot(sel, lo, preferred_element_type=F32)


def _norm_matmul_kernel(x_ref, g_ref, w_ref, o_ref, u_ref):
    @pl.when(pl.program_id(1) == 0)
    def _():
        x = x_ref[...]
        ms = jnp.mean(x * x, axis=-1, keepdims=True)
        u_ref[...] = (x * lax.rsqrt(ms + NORM_EPS) * g_ref[...]).astype(BF16)

    o_ref[...] = jnp.dot(u_ref[...], w_ref[...], preferred_element_type=F32)


def _norm_matmul(x, g, w, tm, tn):
    n, d = x.shape
    m = w.shape[1]
    return pl.pallas_call(
        _norm_matmul_kernel,
        grid=(n // tm, m // tn),
        in_specs=[
            pl.BlockSpec((tm, d), lambda i, j: (i, 0)),
            pl.BlockSpec((1, d), lambda i, j: (0, 0)),
            pl.BlockSpec((d, tn), lambda i, j: (0, j)),
        ],
        out_specs=pl.BlockSpec((tm, tn), lambda i, j: (i, j)),
        out_shape=jax.ShapeDtypeStruct((n, m), F32),
        scratch_shapes=[pltpu.VMEM((tm, d), BF16)],
        compiler_params=_cparams(("parallel", "arbitrary")),
        name="in_proj",
    )(x, g, w)


def _shift_mix(x, prev_ref, mu):
    rows = lax.broadcasted_iota(jnp.int32, x.shape, 0)
    xs = jnp.where(rows == 0, prev_ref[...], pltpu.roll(x, 1, 0))
    prev_ref[...] = x[x.shape[0] - 1:, :]
    return x + (xs - x) * mu


def _rwkv_prep_kernel(has_vres, *refs):
    if has_vres:
        (prkv_ref, plora_ref, vf_ref, mu1_ref, mu2_ref, w0_ref, w2_ref, a0_ref, a2_ref, g2_ref,
         kk_ref, ka_ref, rk_ref, e_ref, et_ref, ctri_ref, csel_ref, v0_ref, v1_ref, v2_ref,
         at_o, rt_o, bt_o, kt_o, vb_o, wl_o, g_o, bonus_o, prev1, prev2) = refs
    else:
        (prkv_ref, plora_ref, mu1_ref, mu2_ref, w0_ref, w2_ref, a0_ref, a2_ref, g2_ref,
         kk_ref, ka_ref, rk_ref, e_ref, et_ref, ctri_ref, csel_ref,
         at_o, rt_o, bt_o, kt_o, vb_o, wl_o, g_o, bonus_o, v_o, prev1, prev2) = refs
    @pl.when(pl.program_id(1) == 0)
    def _():
        prev1[...] = jnp.zeros_like(prev1)
        prev2[...] = jnp.zeros_like(prev2)

    p = _shift_mix(prkv_ref[...], prev1, mu1_ref[...])
    lo = _shift_mix(plora_ref[...], prev2, mu2_ref[...])
    r = p[:, :RW_WIDTH]
    k = p[:, RW_WIDTH:2 * RW_WIDTH]
    v = p[:, 2 * RW_WIDTH:]
    w_log = -_softplus(-(w0_ref[...] + _bdot(jnp.tanh(lo), w2_ref[...]))) - 0.5
    log_w = -jnp.exp(w_log)
    a = _sigmoid(a0_ref[...] + _bdot(lo, a2_ref[...]))
    g = _bdot(_sigmoid(lo), g2_ref[...])
    if has_vres:
        mix = _sigmoid(v0_ref[...] + _bdot(_bdot(v, v1_ref[...]), v2_ref[...]))
        v = v + (vf_ref[...] - v) * mix
    else:
        v_o[...] = v
    e, et = e_ref[...], et_ref[...]
    kk = k * kk_ref[...]
    inv = lax.rsqrt(_pick_dot(kk * kk, e) + 1e-12)
    kk = kk * _pick_dot(inv, et)
    k2 = k * (1.0 + (a - 1.0) * ka_ref[...])
    rk = _pick_dot(_pick_dot(r * k2 * rk_ref[...], e), et)
    cum = _pick_dot_left(ctri_ref[...], log_w)
    grow, shrink = jnp.exp(cum), jnp.exp(-cum)
    at_o[...] = (-kk * jnp.exp(cum - log_w)).astype(BF16)
    rt_o[...] = (r * grow).astype(BF16)
    bt_o[...] = (kk * a * shrink).astype(BF16)
    kt_o[...] = (k2 * shrink).astype(BF16)
    vb_o[...] = v.astype(BF16)
    wl_o[...] = jnp.exp(_pick_dot_left(csel_ref[...], cum))
    g_o[...] = g
    bonus_o[...] = rk * v


def _rwkv_prep(proj3, vfirst, prm, tt):
    b, t, _ = proj3.shape
    has_vres = vfirst is not None
    tok = lambda w, c: pl.BlockSpec((None, tt, w), lambda bi, i: (bi, i, c))
    in_specs = [tok(W_RKV, OFF_RKV // W_RKV), tok(W_LORA, OFF_LORA // W_LORA)]
    args = [proj3, proj3]
    if has_vres:
        in_specs.append(tok(RW_WIDTH, 0))
        args.append(vfirst)
    names = ["mu1", "mu2", "w0", "w2", "a0", "a2", "g2", "k_k", "k_a", "r_k", "seg", "seg_t", "chunk_tri", "chunk_last"]
    if has_vres:
        names += ["v0", "v1", "v2"]
    for nm in names:
        in_specs.append(_full(prm[nm].shape))
        args.append(prm[nm])
    out_shape = [jax.ShapeDtypeStruct((b, t, RW_WIDTH), BF16)] * 5
    out_specs = [tok(RW_WIDTH, 0)] * 5
    out_shape.append(jax.ShapeDtypeStruct((b, t // REC_CHUNK, RW_WIDTH), F32))
    out_specs.append(pl.BlockSpec((None, tt // REC_CHUNK, RW_WIDTH), lambda bi, i: (bi, i, 0)))
    out_shape += [jax.ShapeDtypeStruct((b, t, RW_WIDTH), F32)] * 2
    out_specs += [tok(RW_WIDTH, 0)] * 2
    if not has_vres:
        out_shape.append(jax.ShapeDtypeStruct((b, t, RW_WIDTH), F32))
        out_specs.append(tok(RW_WIDTH, 0))
    return pl.pallas_call(
        functools.partial(_rwkv_prep_kernel, has_vres),
        grid=(b, t // tt),
        in_specs=in_specs,
        out_specs=out_specs,
        out_shape=out_shape,
        scratch_shapes=[pltpu.VMEM((1, W_RKV), F32), pltpu.VMEM((1, W_LORA), F32)],
        compiler_params=_cparams(("parallel", "arbitrary")),
        name="rwkv_prep",
    )(*args)


def _rwkv_chunk_kernel(at_ref, rt_ref, bt_ref, kt_ref, v_ref, wl_ref, y_ref, s_ref):
    nb, tc = at_ref.shape[0], at_ref.shape[1]
    L = REC_CHUNK
    pairs = range(RW_PAIRS)

    @pl.when(pl.program_id(1) == 0)
    def _():
        s_ref[...] = jnp.zeros_like(s_ref)

    lane_l = lax.broadcasted_iota(jnp.int32, (L, LANES), 1)
    row_l = lax.broadcasted_iota(jnp.int32, (L, LANES), 0)
    is_e = lane_l < RW_HEAD
    row_p = lax.broadcasted_iota(jnp.int32, (4 * L, LANES), 0)
    lane_p = lax.broadcasted_iota(jnp.int32, (4 * L, LANES), 1)
    t_p, j_p = row_p % L, lane_p % L
    keep = (lane_p < 2 * L) & ((j_p < t_p) | (((row_p // L) % 2 == 1) & (j_p == t_p)))
    row_s = lax.broadcasted_iota(jnp.int32, (LANES, LANES), 0)
    lane_s = lax.broadcasted_iota(jnp.int32, (LANES, LANES), 1)
    same_head = (row_s < RW_HEAD) == (lane_s < RW_HEAD)
    zeros_b = lambda r: jnp.zeros((r, LANES), BF16)
    mm = lambda a, b: jnp.dot(a, b, preferred_element_type=F32)

    def chunk(j, carry):
        rows = pl.ds(pl.multiple_of(j * L, L), L)
        grp = pl.ds(pl.multiple_of((j // SUBLANES) * SUBLANES, SUBLANES), SUBLANES)
        pick = lax.broadcasted_iota(jnp.int32, (SUBLANES, RW_WIDTH), 0) == j % SUBLANES
        tiles = [(n, p) for n in range(nb) for p in pairs]
        wl_all = [jnp.sum(jnp.where(pick, wl_ref[n, grp, :], 0.0), axis=0, keepdims=True) for n in range(nb)]
        ld = lambda ref: [ref[n, rows, LANES * p:LANES * (p + 1)] for n, p in tiles]
        a, r, b, k, v = ld(at_ref), ld(rt_ref), ld(bt_ref), ld(kt_ref), ld(v_ref)
        ev = lambda m: jnp.where(is_e, m, jnp.zeros_like(m))
        od = lambda m: jnp.where(is_e, jnp.zeros_like(m), m)
        idx = range(len(tiles))
        sc = []
        for i in idx:
            lhs = jnp.concatenate([ev(a[i]), ev(r[i]), od(a[i]), od(r[i])], axis=0)
            rhs = jnp.concatenate([b[i], k[i], zeros_b(LANES - 2 * L)], axis=0)
            s_i = lax.dot_general(lhs, rhs, (((1,), (1,)), ((), ())), preferred_element_type=F32)
            sc.append(jnp.where(keep, s_i, 0.0))
        s0 = [s_ref[n, p] for n, p in tiles]
        x1 = [mm(jnp.concatenate([a[i], r[i]], axis=0), s0[i].astype(BF16)) for i in idx]
        in_e, in_o = lane_l < L, (lane_l >= L) & (lane_l < 2 * L)
        side = lambda m_e, m_o: jnp.where(in_e, m_e, 0.0) + jnp.where(in_o, m_o, 0.0)
        nab = [side(sc[i][:L], pltpu.roll(sc[i][2 * L:3 * L], L, 1)) for i in idx]
        nak = [side(pltpu.roll(sc[i][:L], LANES - L, 1), sc[i][2 * L:3 * L]) for i in idx]
        nr = [sc[i][L:2 * L] + pltpu.roll(sc[i][3 * L:], 2 * L, 1) for i in idx]
        rows_of = lambda *ms: jnp.concatenate([m.astype(BF16) for m in ms] + [zeros_b(LANES - L * len(ms))], axis=0)
        blocks = lambda m: rows_of(jnp.where(in_e, m, 0.0), jnp.where(in_o, m, 0.0))
        x = [x1[i][:L] + mm(nak[i].astype(BF16), rows_of(ev(v[i]), od(v[i]))) for i in idx]
        sq = lambda ms: [mm(m.astype(BF16), blocks(m)) for m in ms]
        p2 = sq(nab)
        p4 = sq(p2)
        eye2 = jnp.where((lane_l == row_l) | (lane_l == row_l + L), 1.0, 0.0)
        tinv = [eye2 + m for m in nab]
        tinv = [t + mm(t.astype(BF16), blocks(q)) for t, q in zip(tinv, p2)]
        tinv = [t + mm(t.astype(BF16), blocks(q)) for t, q in zip(tinv, p4)]
        if L > 8:
            p8 = sq(p4)
            tinv = [t + mm(t.astype(BF16), blocks(q)) for t, q in zip(tinv, p8)]
        sab = [mm(tinv[i].astype(BF16), rows_of(ev(x[i]), od(x[i]))).astype(BF16) for i in idx]
        for i, (n, p) in enumerate(tiles):
            y = x1[i][L:] + mm(nr[i].astype(BF16), rows_of(ev(sab[i]), ev(v[i]), od(sab[i]), od(v[i])))
            y_ref[n, rows, LANES * p:LANES * (p + 1)] = y
        for i, (n, p) in enumerate(tiles):
            upd = lax.dot_general(jnp.concatenate([b[i], k[i]], axis=0), jnp.concatenate([sab[i], v[i]], axis=0),
                                  (((0,), (0,)), ((), ())), preferred_element_type=F32)
            decay = jnp.broadcast_to(wl_all[n][:, LANES * p:LANES * (p + 1)], (LANES, LANES)).T
            s_ref[n, p] = decay * (s0[i] + jnp.where(same_head, upd, 0.0))
        return carry

    lax.fori_loop(0, tc // L, chunk, 0)


def _rwkv_chunk(at, rt, bt, kt, v, wl, tc):
    b, t, c = at.shape
    nb = 4 if b % 4 == 0 else (2 if b % 2 == 0 else 1)
    tok = pl.BlockSpec((nb, tc, c), lambda bi, i: (bi, i, 0))
    return pl.pallas_call(
        _rwkv_chunk_kernel,
        grid=(b // nb, t // tc),
        in_specs=[tok, tok, tok, tok, tok, pl.BlockSpec((nb, tc // REC_CHUNK, c), lambda bi, i: (bi, i, 0))],
        out_specs=tok,
        out_shape=jax.ShapeDtypeStruct((b, t, c), F32),
        scratch_shapes=[pltpu.VMEM((nb, RW_PAIRS, LANES, LANES), F32)],
        compiler_params=_cparams(("parallel", "arbitrary")),
        name="rwkv_chunk",
    )(at, rt, bt, kt, v, wl)


def _rwkv_post_kernel(y_ref, bonus_ref, g_ref, lnw_ref, lnb_ref, e_ref, et_ref, wb_ref, o_ref):
    y = y_ref[...]
    e, et = e_ref[...], et_ref[...]
    mean = _pick_dot(_pick_dot(y, e), et) * (1.0 / RW_HEAD)
    yc = y - mean
    var = _pick_dot(_pick_dot(yc * yc, e), et) * (1.0 / RW_HEAD)
    y = yc * lax.rsqrt(var + RW_GN_EPS) * lnw_ref[...] + lnb_ref[...]
    y = (y + bonus_ref[...]) * g_ref[...]
    o_ref[...] = _bdot(y, wb_ref[...])


def _rwkv_post(y, bonus, g, prm, tt):
    b, t, c = y.shape
    tok = pl.BlockSpec((None, tt, c), lambda bi, i: (bi, i, 0))
    names = ["ln_w", "ln_b", "seg", "seg_t", "w_branch"]
    return pl.pallas_call(
        _rwkv_post_kernel,
        grid=(b, t // tt),
        in_specs=[tok, tok, tok] + [_full(prm[nm].shape) for nm in names],
        out_specs=pl.BlockSpec((None, tt, D_MODEL), lambda bi, i: (bi, i, 0)),
        out_shape=jax.ShapeDtypeStruct((b, t, D_MODEL), F32),
        compiler_params=_cparams(("parallel", "parallel")),
        name="rwkv_post",
    )(y, bonus, g, *[prm[nm] for nm in names])


def _ssd_kernel(z_ref, xbc_ref, dt_ref, cw_ref, cb_ref, dtb_ref, alog_ref, dskip_ref, ng_ref, ex_ref, tri_ref, wb_ref,
                o_ref, buf, st_ref):
    L = SSD_CHUNK
    first = pl.program_id(1) == 0

    @pl.when(first)
    def _():
        buf[0:SUBLANES, :] = jnp.zeros((SUBLANES, SSD_CONV_CH), F32)
        st_ref[...] = jnp.zeros_like(st_ref)

    buf[SUBLANES:, :] = xbc_ref[...]
    window = buf[...]
    conv = cb_ref[...] + window[SUBLANES:, :] * cw_ref[SSD_CONV - 1:SSD_CONV, :]
    for j in range(SSD_CONV - 1):
        lag = SSD_CONV - 1 - j
        conv = conv + pltpu.roll(window, lag, 0)[SUBLANES:, :] * cw_ref[j:j + 1, :]
    buf[0:SUBLANES, :] = window[L:, :]
    xa = conv * _sigmoid(conv)
    xs = xa[:, :SSD_WIDTH]

    dt = _softplus(dt_ref[...] + dtb_ref[...])
    a = dt * (-jnp.exp(alog_ref[...]))
    row = lax.broadcasted_iota(jnp.int32, (L, L), 0)
    col = lax.broadcasted_iota(jnp.int32, (L, L), 1)
    causal = row >= col
    acs = _pick_dot_left(tri_ref[...], a)
    acs_last = acs[L - 1:L, :]
    stacked = jnp.concatenate(
        [dt, jnp.exp(acs), jnp.exp(acs_last - acs), jnp.broadcast_to(jnp.exp(acs_last), (SUBLANES, LANES))], axis=0)
    wide = _pick_dot(stacked, ex_ref[...])
    dt_x, ea_x, dte_x, el_x = wide[:L], wide[L:2 * L], wide[2 * L:3 * L], wide[3 * L:3 * L + 1]
    acs_t = acs.T
    xdt = xs * dt_x
    lane = lax.broadcasted_iota(jnp.int32, (L, LANES), 1)
    is_lo = lane < SSD_HEAD

    y_parts = []
    for g in range(SSD_GROUPS):
        bg = xa[:, SSD_WIDTH + SSD_STATE * g:SSD_WIDTH + SSD_STATE * (g + 1)]
        cg = xa[:, SSD_WIDTH + SSD_STATE * (SSD_GROUPS + g):SSD_WIDTH + SSD_STATE * (SSD_GROUPS + g + 1)]
        cb = lax.dot_general(cg.astype(BF16), bg.astype(BF16), (((1,), (1,)), ((), ())), preferred_element_type=F32)
        gl = slice(SSD_GW * g, SSD_GW * (g + 1))
        y_off = _bdot(cg, st_ref[g]) * ea_x[:, gl]
        diag = []
        for pp in range(SSD_HG // 2):
            acc = None
            xp = xdt[:, SSD_GW * g + LANES * pp:SSD_GW * g + LANES * (pp + 1)]
            for half in range(2):
                h = SSD_HG * g + 2 * pp + half
                seg = acs[:, h:h + 1] - acs_t[h:h + 1, :]
                lm = jnp.exp(jnp.where(causal, seg, -BIG))
                xh = jnp.where(is_lo if half == 0 else jnp.logical_not(is_lo), xp, 0.0)
                term = _bdot(lm * cb, xh)
                acc = term if acc is None else acc + term
            diag.append(acc)
        y_parts.append(jnp.concatenate(diag, axis=1) + y_off)
        st_ref[g] = st_ref[g] * el_x[:, gl] + _bdot(bg.T, xdt[:, gl] * dte_x[:, gl])
    y = jnp.concatenate(y_parts, axis=1)
    y = y + xs * dskip_ref[...]
    z = z_ref[...]
    y = y * (z * _sigmoid(z))
    normed = []
    for g in range(SSD_GROUPS):
        yg = y[:, SSD_GW * g:SSD_GW * (g + 1)]
        ms = jnp.mean(yg * yg, axis=-1, keepdims=True)
        normed.append(yg * lax.rsqrt(ms + NORM_EPS))
    y = jnp.concatenate(normed, axis=1) * ng_ref[...]
    o_ref[...] = _bdot(y, wb_ref[...])


def _ssd(proj3, prm):
    b, t, _ = proj3.shape
    L = SSD_CHUNK
    tok = lambda w, off: pl.BlockSpec((None, L, w), lambda bi, i: (bi, i, off // w))
    names = ["conv_w", "conv_b", "dt_bias", "a_log", "d_skip", "norm_g", "expand", "tri", "w_branch"]
    return pl.pallas_call(
        _ssd_kernel,
        grid=(b, t // L),
        in_specs=[tok(W_Z, OFF_Z), tok(W_XBC, OFF_XBC), tok(W_DT, OFF_DT)] + [_full(prm[nm].shape) for nm in names],
        out_specs=pl.BlockSpec((None, L, D_MODEL), lambda bi, i: (bi, i, 0)),
        out_shape=jax.ShapeDtypeStruct((b, t, D_MODEL), F32),
        scratch_shapes=[pltpu.VMEM((L + SUBLANES, SSD_CONV_CH), F32), pltpu.VMEM((SSD_GROUPS, SSD_STATE, SSD_GW), F32)],
        compiler_params=_cparams(("parallel", "arbitrary")),
        name="ssd",
    )(proj3, proj3, proj3, *[prm[nm] for nm in names])


def _mix_kernel(ya_ref, yb_ref, gate_ref, h_ref, wo_ref, gf_ref, h_o, ut_o):
    gt = gate_ref[...]
    mixed = _sigmoid(gt[:, :D_MODEL]) * ya_ref[...] + _sigmoid(gt[:, D_MODEL:]) * yb_ref[...]
    h = h_ref[...] + _bdot(mixed, wo_ref[...])
    h_o[...] = h
    ms = jnp.mean(h * h, axis=-1, keepdims=True)
    u = h * lax.rsqrt(ms + NORM_EPS) * gf_ref[...]
    ut_o[...] = pltpu.bitcast(u.T.astype(BF16), jnp.uint32)


def _mix(ya, yb, proj, h, prm, tm):
    n = h.shape[0]
    names = ["w_out", "norm_ffn_g"]
    tok = lambda w, c: pl.BlockSpec((tm, w), lambda i: (i, c))
    return pl.pallas_call(
        _mix_kernel,
        grid=(n // tm,),
        in_specs=[tok(D_MODEL, 0), tok(D_MODEL, 0), tok(W_GATE, OFF_GATE // W_GATE), tok(D_MODEL, 0)]
        + [_full(prm[nm].shape) for nm in names],
        out_specs=[tok(D_MODEL, 0), pl.BlockSpec((D_MODEL // 2, tm), lambda i: (0, i))],
        out_shape=[jax.ShapeDtypeStruct((n, D_MODEL), F32), jax.ShapeDtypeStruct((D_MODEL // 2, n), jnp.uint32)],
        compiler_params=_cparams(("parallel",)),
        name="mix",
    )(ya, yb, proj, h, *[prm[nm] for nm in names])


def _top_rows(x, n, want_rank):
    tm = x.shape[1]
    row_n = lax.broadcasted_iota(jnp.int32, (n, tm), 0)
    tops = jnp.zeros((n, tm), F32)
    rank = jnp.full(x.shape, float(n), F32) if want_rank else None
    m = None
    for k in range(n):
        m = jnp.max(x, axis=0, keepdims=True)
        tops = jnp.where(row_n == k, m, tops)
        hit = x >= m
        if want_rank:
            rank = jnp.where(hit, float(k), rank)
        if k + 1 < n:
            x = jnp.where(hit, -BIG, x)
    return tops, m, rank


def _peer_sel_kernel(ut_ref, wq_ref, keys_ref, e1_o, cnt_o, e2_o, r2_o):
    k = PEER_TOPK
    qt = jnp.dot(wq_ref[...], pltpu.bitcast(ut_ref[...], BF16), preferred_element_type=F32)
    for h in range(PEER_HEADS):
        s = []
        for c in range(2):
            j = 2 * h + c
            s.append(_bdot(keys_ref[j], qt[LANES * j:LANES * (j + 1), :]))
        s1, s2 = s
        top1, thr1, _ = _top_rows(s1, k, False)
        top2, _, rank2 = _top_rows(s2, k, True)
        in1 = s1 >= thr1
        in2 = rank2 < float(k)
        row8 = lax.broadcasted_iota(jnp.int32, (SUBLANES, s1.shape[1]), 0)
        groups = [top1[0:1, :] + top2]
        for k1 in range(1, SUBLANES):
            width = k // (k1 + 1)
            groups.append(jnp.where(row8 < width, top1[k1:k1 + 1, :] + top2[:SUBLANES, :], -BIG))
        groups.append(top1[SUBLANES:, :] + top2[0:1, :])
        x = jnp.concatenate(groups, axis=0)
        m0 = zsum = m = None
        for i in range(k):
            m = jnp.max(x, axis=0, keepdims=True)
            if i == 0:
                m0, zsum = m, jnp.ones_like(m)
            else:
                zsum = zsum + jnp.exp(m - m0)
            if i + 1 < k:
                x = jnp.where(x >= m, -BIG, x)
        thr = m
        cnt_rank = jnp.zeros_like(top1)
        for k2 in range(k):
            cnt_rank = cnt_rank + jnp.where(top1 + top2[k2:k2 + 1, :] >= thr, 1.0, 0.0)
        cnt = jnp.zeros_like(s1)
        for k1 in range(k):
            cnt = jnp.where(s1 == top1[k1:k1 + 1, :], cnt_rank[k1:k1 + 1, :], cnt)
        e1_o[h] = jnp.where(in1, jnp.exp(s1 - top1[0:1, :]), 0.0) / zsum
        cnt_o[h] = cnt
        e2_o[h] = pltpu.bitcast(jnp.where(in2, jnp.exp(s2 - top2[0:1, :]), 0.0).astype(BF16), jnp.uint32)
        r2_o[h] = pltpu.bitcast(rank2.astype(BF16), jnp.uint32)


def _peer_sel(ut, prm, tm):
    n = ut.shape[1]
    shp = (PEER_HEADS, PEER_NKEYS, n)
    ospec = pl.BlockSpec((PEER_HEADS, PEER_NKEYS, tm), lambda i: (0, 0, i))
    pshp = (PEER_HEADS, PEER_NKEYS // 2, n)
    pspec = pl.BlockSpec((PEER_HEADS, PEER_NKEYS // 2, tm), lambda i: (0, 0, i))
    return pl.pallas_call(
        _peer_sel_kernel,
        grid=(n // tm,),
        in_specs=[pl.BlockSpec((D_MODEL // 2, tm), lambda i: (0, i)), _full(prm["wq_t"].shape), _full(prm["keys"].shape)],
        out_specs=[ospec, ospec, pspec, pspec],
        out_shape=[jax.ShapeDtypeStruct(shp, F32), jax.ShapeDtypeStruct(shp, F32),
                   jax.ShapeDtypeStruct(pshp, jnp.uint32), jax.ShapeDtypeStruct(pshp, jnp.uint32)],
        compiler_params=_cparams(("parallel",)),
        name="peer_sel",
    )(ut, prm["wq_t"], prm["keys"])


def _gated_gelu(x, gate):
    c = 0.7978845608028654
    t = jnp.tanh(x * (c + (c * 0.044715) * (x * x)))
    hx = x.astype(BF16) * 0.5
    return (hx + hx * t.astype(BF16)) * gate


def _peer_ffn_kernel(final_g, ut_ref, down_ref, upt_ref, uptp_ref, e1_ref, cnt_ref, e2_ref, r2_ref, h_ref, *rest):
    if final_g:
        gfin_ref, o_ref, *scr = rest
    else:
        o_ref, *scr = rest
    nh = len(scr) // 5
    acc, act, gbuf, hs, tok = (scr[nh * q:nh * (q + 1)] for q in range(5))
    j = pl.program_id(1)
    te, th = act[0].shape
    sub = 2 * SUBLANES

    @pl.when(j == 0)
    def _():
        for hf in range(nh):
            acc[hf][...] = jnp.zeros_like(acc[hf])
            tok[hf][...] = pltpu.bitcast(ut_ref[:, th * hf:th * (hf + 1)], BF16)
        hs[nh - 1][...] = jnp.zeros_like(hs[nh - 1])

    kb = 1
    blk = kb * PEER_NKEYS
    nblk = te // blk
    nm = D_MODEL // blk
    sbt = 4

    def project_down(hf, b):
        rows = slice(blk * b, blk * (b + 1))
        lhs = pltpu.bitcast(down_ref[blk // 2 * b:blk // 2 * (b + 1), :], BF16)
        act[hf][rows, :] = jnp.dot(lhs, tok[hf][...], preferred_element_type=F32)

    def build_gates(hf, b):
        for c in range(th // LANES):
            cl = slice(th * hf + LANES * c, th * hf + LANES * (c + 1))
            first = lambda ref, h, q: jnp.broadcast_to(ref[h, kb * b + q:kb * b + q + 1, cl], (sub, LANES)).astype(BF16)
            cnt = [[first(cnt_ref, h, q) for q in range(kb)] for h in range(PEER_HEADS)]
            e1 = [[first(e1_ref, h, q) for q in range(kb)] for h in range(PEER_HEADS)]
            for s0 in range(0, PEER_NKEYS // sub, sbt):
                gate = [[jnp.zeros((sub, LANES), BF16) for _ in range(sbt)] for _ in range(kb)]
                for h in range(PEER_HEADS):
                    r2 = [pltpu.bitcast(r2_ref[h, SUBLANES * (s0 + s):SUBLANES * (s0 + s + 1), cl], BF16) for s in range(sbt)]
                    e2 = [pltpu.bitcast(e2_ref[h, SUBLANES * (s0 + s):SUBLANES * (s0 + s + 1), cl], BF16) for s in range(sbt)]
                    for q in range(kb):
                        for s in range(sbt):
                            gate[q][s] = gate[q][s] + jnp.where(r2[s] < cnt[h][q], e2[s], 0.0) * e1[h][q]
                for q in range(kb):
                    for s in range(sbt):
                        gw = (blk * b + PEER_NKEYS * q + sub * (s0 + s)) // 2
                        gbuf[hf][gw:gw + SUBLANES, LANES * c:LANES * (c + 1)] = pltpu.bitcast(gate[q][s], jnp.uint32)

    def activate(hf, b):
        for r0 in range(blk * b, blk * (b + 1), sub):
            for c in range(th // LANES):
                cl = slice(LANES * c, LANES * (c + 1))
                gate = pltpu.bitcast(gbuf[hf][r0 // 2:r0 // 2 + SUBLANES, cl], BF16)
                hs[hf][r0:r0 + sub, cl] = _gated_gelu(act[hf][r0:r0 + sub, cl], gate)

    def project_up(w_ref, hf, m):
        rows = slice(blk * m, blk * (m + 1))
        lhs = pltpu.bitcast(w_ref[blk // 2 * m:blk // 2 * (m + 1), :], BF16)
        acc[hf][rows, :] += jnp.dot(lhs, hs[hf][...], preferred_element_type=F32)

    for hf in range(nh):
        for b in range(nblk):
            project_down(hf, b)
            for m in range(b * nm // nblk, (b + 1) * nm // nblk):
                if hf == 0:
                    project_up(uptp_ref, nh - 1, m)
                else:
                    project_up(upt_ref, hf - 1, m)
            build_gates(hf, b)
            if b > 0:
                activate(hf, b - 1)
        activate(hf, nblk - 1)

    @pl.when(j == pl.num_programs(1) - 1)
    def _():
        for m in range(nm):
            project_up(upt_ref, nh - 1, m)
        for hf in range(nh):
            h = h_ref[th * hf:th * (hf + 1), :] + acc[hf][...].T
            if final_g:
                ms = jnp.mean(h * h, axis=-1, keepdims=True)
                h = h * lax.rsqrt(ms + NORM_EPS) * gfin_ref[...]
            o_ref[th * hf:th * (hf + 1), :] = h


def _peer_ffn(ut, down, upt, e1, cnt, e2, r2, h, final_g, tl, te):
    n = h.shape[0]
    nh = 2 if tl % (2 * LANES) == 0 else 1
    sel = pl.BlockSpec((PEER_HEADS, PEER_NKEYS // 2, tl), lambda i, j: (0, 0, i))
    sel1 = pl.BlockSpec((PEER_HEADS, te // PEER_NKEYS, tl), lambda i, j: (0, j, i))
    in_specs = [
        pl.BlockSpec((D_MODEL // 2, tl), lambda i, j: (0, i)),
        pl.BlockSpec((te // 2, D_MODEL), lambda i, j: (j, 0)),
        pl.BlockSpec((D_MODEL // 2, te), lambda i, j: (0, j)),
        pl.BlockSpec((D_MODEL // 2, te), lambda i, j: (0, jnp.maximum(j - 1, 0))),
        sel1, sel1, sel, sel,
        pl.BlockSpec((tl, D_MODEL), lambda i, j: (i, 0)),
    ]
    args = [ut, down, upt, upt, e1, cnt, e2, r2, h]
    if final_g is not None:
        in_specs.append(_full(final_g.shape))
        args.append(final_g)
    return pl.pallas_call(
        functools.partial(_peer_ffn_kernel, final_g is not None),
        grid=(n // tl, PEER_EXPERTS // te),
        in_specs=in_specs,
        out_specs=pl.BlockSpec((tl, D_MODEL), lambda i, j: (i, 0)),
        out_shape=jax.ShapeDtypeStruct((n, D_MODEL), F32),
        scratch_shapes=([pltpu.VMEM((D_MODEL, tl // nh), F32)] * nh + [pltpu.VMEM((te, tl // nh), F32)] * nh
                        + [pltpu.VMEM((te // 2, tl // nh), jnp.uint32)] * nh + [pltpu.VMEM((te, tl // nh), BF16)] * nh
                        + [pltpu.VMEM((D_MODEL, tl // nh), BF16)] * nh),
        compiler_params=_cparams(("parallel", "arbitrary")),
        name="peer_ffn",
    )(*args)


def _pack_weight_kernel(transpose, w_ref, o_ref):
    w = w_ref[...]
    if transpose:
        w = w.T
    o_ref[...] = pltpu.bitcast(w.astype(BF16), jnp.uint32)


def _pack_weight(w, transpose):
    r, c = w.shape
    t = D_MODEL
    out_shape, out_map = ((c // 2, r), lambda i, j: (j, i)) if transpose else ((r // 2, c), lambda i, j: (i, j))
    return pl.pallas_call(
        functools.partial(_pack_weight_kernel, transpose),
        grid=(r // t, c // t),
        in_specs=[pl.BlockSpec((t, t), lambda i, j: (i, j))],
        out_specs=pl.BlockSpec((t // 2, t), out_map),
        out_shape=jax.ShapeDtypeStruct(out_shape, jnp.uint32),
        compiler_params=_cparams(("parallel", "parallel")),
        name="pack_weight",
    )(w)


def _pad_rows(w, start, total):
    return jnp.zeros((total, w.shape[1]), w.dtype).at[start:start + w.shape[0]].set(w)


def _pad_lanes(v, total):
    return jnp.zeros((1, total), v.dtype).at[0, :v.shape[0]].set(v)


def _row(v):
    return v.reshape(1, -1)


def _constants():
    c = jnp.arange(RW_WIDTH)[:, None] // RW_HEAD == jnp.arange(LANES)[None, :]
    seg = c.astype(BF16)
    expand = (jnp.arange(LANES)[:, None] == jnp.arange(SSD_WIDTH)[None, :] // SSD_HEAD).astype(BF16)
    return seg, seg.T, expand


def kernel(x, norm_mix_g, w_in, rwkv_mu, rwkv_w0, rwkv_w2, rwkv_a0, rwkv_a2, rwkv_g2, rwkv_v0, rwkv_v1, rwkv_v2,
           rwkv_k_k, rwkv_k_a, rwkv_r_k, rwkv_ln_w, rwkv_ln_b, w_rwkv_branch, ssd_conv_w, ssd_conv_b, ssd_dt_bias,
           ssd_A_log, ssd_D, ssd_norm_g, w_ssd_branch, w_out, norm_ffn_g, peer_w_q, peer_sub_keys, peer_down, peer_up,
           norm_final_g):
    bsz, t, d = x.shape
    n = bsz * t
    depth = w_in.shape[0]
    seg, seg_t, expand = _constants()
    h = x.reshape(n, d)
    v_first = None
    o_lora = 3 * RW_WIDTH
    o_z = o_lora + RW_LORA
    o_xbc = o_z + SSD_WIDTH
    o_dt = o_xbc + SSD_CONV_CH
    o_gate = o_dt + SSD_HEADS
    tt = min(TOK_TILE, t)
    step = jnp.arange(tt)
    chunk_tri = ((step[:, None] // REC_CHUNK == step[None, :] // REC_CHUNK) & (step[:, None] >= step[None, :])).astype(BF16)
    chunk_last = (jnp.arange(tt // REC_CHUNK)[:, None] * REC_CHUNK + REC_CHUNK - 1 == step[None, :]).astype(BF16)
    ssd_tri = (jnp.arange(SSD_CHUNK)[:, None] >= jnp.arange(SSD_CHUNK)[None, :]).astype(BF16)
    for l in range(depth):
        wl = w_in[l]
        zpad = lambda k: jnp.zeros((d, k), wl.dtype)
        w_cat = jnp.concatenate([
            wl[:, :o_lora], wl[:, o_xbc:o_dt], wl[:, o_z:o_xbc], wl[:, o_gate:],
            wl[:, o_dt:o_gate], zpad(LANES - SSD_HEADS), wl[:, o_lora:o_z], zpad(RW_LORA_PAD - RW_LORA)], axis=1).astype(BF16)
        proj = _norm_matmul(h, _row(norm_mix_g[l]), w_cat, tm=min(PROJ_TILE[0], n), tn=PROJ_TILE[1])
        proj3 = proj.reshape(bsz, t, PROJ_PAD)

        mu = rwkv_mu[l]
        rw = {
            "mu1": _row(mu[:o_lora]), "mu2": _pad_lanes(mu[o_lora:], RW_LORA_PAD),
            "w0": _row(rwkv_w0[l]), "w2": _pad_rows(rwkv_w2[l], 0, RW_LORA_PAD).astype(BF16),
            "a0": _row(rwkv_a0[l]), "a2": _pad_rows(rwkv_a2[l], DECAY_LORA, RW_LORA_PAD).astype(BF16),
            "g2": _pad_rows(rwkv_g2[l], DECAY_LORA + ICLR_LORA, RW_LORA_PAD).astype(BF16),
            "k_k": _row(rwkv_k_k[l]), "k_a": _row(rwkv_k_a[l]), "r_k": _row(rwkv_r_k[l]),
            "seg": seg, "seg_t": seg_t, "chunk_tri": chunk_tri, "chunk_last": chunk_last,
            "ln_w": _row(rwkv_ln_w[l]), "ln_b": _row(rwkv_ln_b[l]), "w_branch": w_rwkv_branch[l].astype(BF16),
        }
        if l > 0:
            rw["v0"] = _row(rwkv_v0[l - 1])
            rw["v1"] = jnp.zeros((RW_WIDTH, VRES_PAD), F32).at[:, :VRES_LORA].set(rwkv_v1[l - 1]).astype(BF16)
            rw["v2"] = _pad_rows(rwkv_v2[l - 1], 0, VRES_PAD).astype(BF16)
        outs = _rwkv_prep(proj3, v_first, rw, tt)
        at, rt, bt, kt, vb, w_end, g_, bonus = outs[:8]
        if l == 0:
            v_first = outs[8]
        y_r = _rwkv_chunk(at, rt, bt, kt, vb, w_end, tc=tt)
        y_a = _rwkv_post(y_r, bonus, g_, rw, tt)

        sp = {
            "conv_w": ssd_conv_w[l], "conv_b": _row(ssd_conv_b[l]),
            "dt_bias": _pad_lanes(ssd_dt_bias[l], LANES), "a_log": _pad_lanes(ssd_A_log[l], LANES),
            "d_skip": _row(jnp.repeat(ssd_D[l], SSD_HEAD)), "norm_g": _row(ssd_norm_g[l]),
            "expand": expand, "tri": ssd_tri, "w_branch": w_ssd_branch[l].astype(BF16),
        }
        y_b = _ssd(proj3, sp)

        pp = {
            "w_out": w_out[l].astype(BF16), "norm_ffn_g": _row(norm_ffn_g[l]), "wq_t": peer_w_q[l].T.astype(BF16),
            "keys": peer_sub_keys[l].reshape(2 * PEER_HEADS, PEER_NKEYS, PEER_QDIM // 2).astype(BF16),
        }
        h, ut = _mix(y_a.reshape(n, d), y_b.reshape(n, d), proj, h, pp, tm=min(TOK_TILE, n))
        e1, cnt, e2, r2 = _peer_sel(ut, pp, tm=min(TOK_TILE, n))
        fin = _row(norm_final_g) if l == depth - 1 else None
        h = _peer_ffn(ut, _pack_weight(peer_down[l], False), _pack_weight(peer_up[l], True), e1, cnt, e2, r2, h, fin,
                      tl=min(FFN_TILE[0], n), te=FFN_TILE[1])
    return h.reshape(bsz, t, d)
```

```python
import functools

import jax
import jax.numpy as jnp
from jax import lax
from jax.experimental import pallas as pl
from jax.experimental.pallas import tpu as pltpu

F32 = jnp.float32
BF16 = jnp.bfloat16

LANES = 128
SUBLANES = 8
NORM_EPS = 1e-6

D_MODEL = 1024
RW_HEAD = 64
RW_WIDTH = D_MODEL
RW_HEADS = RW_WIDTH // RW_HEAD
RW_PAIRS = RW_WIDTH // LANES
DECAY_LORA, ICLR_LORA, GATE_LORA, VRES_LORA = 64, 64, 160, 32
RW_LORA = DECAY_LORA + ICLR_LORA + GATE_LORA
RW_LORA_PAD = 384
VRES_PAD = 128
RW_GN_EPS = 64e-5
REC_CHUNK = 16
SSD_WIDTH = 2 * D_MODEL
SSD_HEAD = 64
SSD_HEADS = SSD_WIDTH // SSD_HEAD
SSD_STATE = 128
SSD_GROUPS = 4
SSD_GW = SSD_WIDTH // SSD_GROUPS
SSD_HG = SSD_HEADS // SSD_GROUPS
SSD_CONV = 4
SSD_CONV_CH = SSD_WIDTH + 2 * SSD_GROUPS * SSD_STATE
SSD_CHUNK = 128
PEER_HEADS = 8
PEER_NKEYS = 128
PEER_TOPK = 16
PEER_QDIM = 256
PEER_EXPERTS = PEER_NKEYS * PEER_NKEYS
BIG = 1e30

OFF_RKV, W_RKV = 0, 3 * RW_WIDTH
OFF_XBC, W_XBC = 3072, SSD_CONV_CH
OFF_Z, W_Z = 6144, SSD_WIDTH
OFF_GATE, W_GATE = 8192, 2 * D_MODEL
OFF_DT, W_DT = 10240, LANES
OFF_LORA, W_LORA = 10368, RW_LORA_PAD
PROJ_PAD = 10752

VMEM_LIMIT = 56 * 1024 * 1024
TOK_TILE = 256
PROJ_TILE = (1024, 1792)
FFN_TILE = (512, 8 * PEER_NKEYS)


def _cparams(sem):
    return pltpu.CompilerParams(dimension_semantics=sem, vmem_limit_bytes=VMEM_LIMIT)


def _full(shape):
    n = len(shape)
    return pl.BlockSpec(shape, lambda *_: (0,) * n)


def _sigmoid(x):
    return 1.0 / (1.0 + jnp.exp(-x))


def _softplus(x):
    return jnp.maximum(x, 0.0) + jnp.log(1.0 + jnp.exp(-jnp.abs(x)))


def _bdot(a, b):
    return jnp.dot(a.astype(BF16), b.astype(BF16), preferred_element_type=F32)


def _split(x):
    hi = x.astype(BF16)
    return hi, (x - hi.astype(F32)).astype(BF16)


def _pick_dot(x, sel):
    hi, lo = _split(x)
    return jnp.dot(hi, sel, preferred_element_type=F32) + jnp.dot(lo, sel, preferred_element_type=F32)


def _pick_dot_left(sel, x):
    hi, lo = _split(x)
    return jnp.dot(sel, hi, preferred_element_type=F32) + jnp.dot(sel, lo, preferred_element_type=F32)


def _norm_matmul_kernel(x_ref, g_ref, w_ref, o_ref, u_ref):
    @pl.when(pl.program_id(1) == 0)
    def _():
        x = x_ref[...]
        ms = jnp.mean(x * x, axis=-1, keepdims=True)
        u_ref[...] = (x * lax.rsqrt(ms + NORM_EPS) * g_ref[...]).astype(BF16)

    o_ref[...] = jnp.dot(u_ref[...], w_ref[...], preferred_element_type=F32)


def _norm_matmul(x, g, w, tm, tn):
    n, d = x.shape
    m = w.shape[1]
    return pl.pallas_call(
        _norm_matmul_kernel,
        grid=(n // tm, m // tn),
        in_specs=[
            pl.BlockSpec((tm, d), lambda i, j: (i, 0)),
            pl.BlockSpec((1, d), lambda i, j: (0, 0)),
            pl.BlockSpec((d, tn), lambda i, j: (0, j)),
        ],
        out_specs=pl.BlockSpec((tm, tn), lambda i, j: (i, j)),
        out_shape=jax.ShapeDtypeStruct((n, m), F32),
        scratch_shapes=[pltpu.VMEM((tm, d), BF16)],
        compiler_params=_cparams(("parallel", "arbitrary")),
        name="in_proj",
    )(x, g, w)


def _shift_mix(x, prev_ref, mu):
    rows = lax.broadcasted_iota(jnp.int32, x.shape, 0)
    xs = jnp.where(rows == 0, prev_ref[...], pltpu.roll(x, 1, 0))
    prev_ref[...] = x[x.shape[0] - 1:, :]
    return x + (xs - x) * mu


def _rwkv_prep_kernel(has_vres, *refs):
    if has_vres:
        (prkv_ref, plora_ref, vf_ref, mu1_ref, mu2_ref, w0_ref, w2_ref, a0_ref, a2_ref, g2_ref,
         kk_ref, ka_ref, rk_ref, e_ref, et_ref, ctri_ref, csel_ref, v0_ref, v1_ref, v2_ref,
         at_o, rt_o, bt_o, kt_o, vb_o, wl_o, g_o, bonus_o, prev1, prev2) = refs
    else:
        (prkv_ref, plora_ref, mu1_ref, mu2_ref, w0_ref, w2_ref, a0_ref, a2_ref, g2_ref,
         kk_ref, ka_ref, rk_ref, e_ref, et_ref, ctri_ref, csel_ref,
         at_o, rt_o, bt_o, kt_o, vb_o, wl_o, g_o, bonus_o, v_o, prev1, prev2) = refs
    @pl.when(pl.program_id(1) == 0)
    def _():
        prev1[...] = jnp.zeros_like(prev1)
        prev2[...] = jnp.zeros_like(prev2)

    p = _shift_mix(prkv_ref[...], prev1, mu1_ref[...])
    lo = _shift_mix(plora_ref[...], prev2, mu2_ref[...])
    r = p[:, :RW_WIDTH]
    k = p[:, RW_WIDTH:2 * RW_WIDTH]
    v = p[:, 2 * RW_WIDTH:]
    w_log = -_softplus(-(w0_ref[...] + _bdot(jnp.tanh(lo), w2_ref[...]))) - 0.5
    log_w = -jnp.exp(w_log)
    a = _sigmoid(a0_ref[...] + _bdot(lo, a2_ref[...]))
    g = _bdot(_sigmoid(lo), g2_ref[...])
    if has_vres:
        mix = _sigmoid(v0_ref[...] + _bdot(_bdot(v, v1_ref[...]), v2_ref[...]))
        v = v + (vf_ref[...] - v) * mix
    else:
        v_o[...] = v
    e, et = e_ref[...], et_ref[...]
    kk = k * kk_ref[...]
    inv = lax.rsqrt(_pick_dot(kk * kk, e) + 1e-12)
    kk = kk * _pick_dot(inv, et)
    k2 = k * (1.0 + (a - 1.0) * ka_ref[...])
    rk = _pick_dot(_pick_dot(r * k2 * rk_ref[...], e), et)
    cum = _pick_dot_left(ctri_ref[...], log_w)
    grow, shrink = jnp.exp(cum), jnp.exp(-cum)
    at_o[...] = (-kk * jnp.exp(cum - log_w)).astype(BF16)
    rt_o[...] = (r * grow).astype(BF16)
    bt_o[...] = (kk * a * shrink).astype(BF16)
    kt_o[...] = (k2 * shrink).astype(BF16)
    vb_o[...] = v.astype(BF16)
    wl_o[...] = jnp.exp(_pick_dot_left(csel_ref[...], cum))
    g_o[...] = g
    bonus_o[...] = rk * v


def _rwkv_prep(proj3, vfirst, prm, tt):
    b, t, _ = proj3.shape
    has_vres = vfirst is not None
    tok = lambda w, c: pl.BlockSpec((None, tt, w), lambda bi, i: (bi, i, c))
    in_specs = [tok(W_RKV, OFF_RKV // W_RKV), tok(W_LORA, OFF_LORA // W_LORA)]
    args = [proj3, proj3]
    if has_vres:
        in_specs.append(tok(RW_WIDTH, 0))
        args.append(vfirst)
    names = ["mu1", "mu2", "w0", "w2", "a0", "a2", "g2", "k_k", "k_a", "r_k", "seg", "seg_t", "chunk_tri", "chunk_last"]
    if has_vres:
        names += ["v0", "v1", "v2"]
    for nm in names:
        in_specs.append(_full(prm[nm].shape))
        args.append(prm[nm])
    out_shape = [jax.ShapeDtypeStruct((b, t, RW_WIDTH), BF16)] * 5
    out_specs = [tok(RW_WIDTH, 0)] * 5
    out_shape.append(jax.ShapeDtypeStruct((b, t // REC_CHUNK, RW_WIDTH), F32))
    out_specs.append(pl.BlockSpec((None, tt // REC_CHUNK, RW_WIDTH), lambda bi, i: (bi, i, 0)))
    out_shape += [jax.ShapeDtypeStruct((b, t, RW_WIDTH), F32)] * 2
    out_specs += [tok(RW_WIDTH, 0)] * 2
    if not has_vres:
        out_shape.append(jax.ShapeDtypeStruct((b, t, RW_WIDTH), F32))
        out_specs.append(tok(RW_WIDTH, 0))
    return pl.pallas_call(
        functools.partial(_rwkv_prep_kernel, has_vres),
        grid=(b, t // tt),
        in_specs=in_specs,
        out_specs=out_specs,
        out_shape=out_shape,
        scratch_shapes=[pltpu.VMEM((1, W_RKV), F32), pltpu.VMEM((1, W_LORA), F32)],
        compiler_params=_cparams(("parallel", "arbitrary")),
        name="rwkv_prep",
    )(*args)


def _rwkv_chunk_kernel(at_ref, rt_ref, bt_ref, kt_ref, v_ref, wl_ref, y_ref, s_ref):
    nb, tc = at_ref.shape[0], at_ref.shape[1]
    L = REC_CHUNK
    pairs = range(RW_PAIRS)

    @pl.when(pl.program_id(1) == 0)
    def _():
        s_ref[...] = jnp.zeros_like(s_ref)

    lane_l = lax.broadcasted_iota(jnp.int32, (L, LANES), 1)
    row_l = lax.broadcasted_iota(jnp.int32, (L, LANES), 0)
    is_e = lane_l < RW_HEAD
    row_p = lax.broadcasted_iota(jnp.int32, (4 * L, LANES), 0)
    lane_p = lax.broadcasted_iota(jnp.int32, (4 * L, LANES), 1)
    t_p, j_p = row_p % L, lane_p % L
    keep = (lane_p < 2 * L) & ((j_p < t_p) | (((row_p // L) % 2 == 1) & (j_p == t_p)))
    row_s = lax.broadcasted_iota(jnp.int32, (LANES, LANES), 0)
    lane_s = lax.broadcasted_iota(jnp.int32, (LANES, LANES), 1)
    same_head = (row_s < RW_HEAD) == (lane_s < RW_HEAD)
    zeros_b = lambda r: jnp.zeros((r, LANES), BF16)
    mm = lambda a, b: jnp.dot(a, b, preferred_element_type=F32)

    def chunk(j, carry):
        rows = pl.ds(pl.multiple_of(j * L, L), L)
        grp = pl.ds(pl.multiple_of((j // SUBLANES) * SUBLANES, SUBLANES), SUBLANES)
        pick = lax.broadcasted_iota(jnp.int32, (SUBLANES, RW_WIDTH), 0) == j % SUBLANES
        tiles = [(n, p) for n in range(nb) for p in pairs]
        wl_all = [jnp.sum(jnp.where(pick, wl_ref[n, grp, :], 0.0), axis=0, keepdims=True) for n in range(nb)]
        ld = lambda ref: [ref[n, rows, LANES * p:LANES * (p + 1)] for n, p in tiles]
        a, r, b, k, v = ld(at_ref), ld(rt_ref), ld(bt_ref), ld(kt_ref), ld(v_ref)
        ev = lambda m: jnp.where(is_e, m, jnp.zeros_like(m))
        od = lambda m: jnp.where(is_e, jnp.zeros_like(m), m)
        idx = range(len(tiles))
        sc = []
        for i in idx:
            lhs = jnp.concatenate([ev(a[i]), ev(r[i]), od(a[i]), od(r[i])], axis=0)
            rhs = jnp.concatenate([b[i], k[i], zeros_b(LANES - 2 * L)], axis=0)
            s_i = lax.dot_general(lhs, rhs, (((1,), (1,)), ((), ())), preferred_element_type=F32)
            sc.append(jnp.where(keep, s_i, 0.0))
        s0 = [s_ref[n, p] for n, p in tiles]
        x1 = [mm(jnp.concatenate([a[i], r[i]], axis=0), s0[i].astype(BF16)) for i in idx]
        in_e, in_o = lane_l < L, (lane_l >= L) & (lane_l < 2 * L)
        side = lambda m_e, m_o: jnp.where(in_e, m_e, 0.0) + jnp.where(in_o, m_o, 0.0)
        nab = [side(sc[i][:L], pltpu.roll(sc[i][2 * L:3 * L], L, 1)) for i in idx]
        nak = [side(pltpu.roll(sc[i][:L], LANES - L, 1), sc[i][2 * L:3 * L]) for i in idx]
        nr = [sc[i][L:2 * L] + pltpu.roll(sc[i][3 * L:], 2 * L, 1) for i in idx]
        rows_of = lambda *ms: jnp.concatenate([m.astype(BF16) for m in ms] + [zeros_b(LANES - L * len(ms))], axis=0)
        blocks = lambda m: rows_of(jnp.where(in_e, m, 0.0), jnp.where(in_o, m, 0.0))
        x = [x1[i][:L] + mm(nak[i].astype(BF16), rows_of(ev(v[i]), od(v[i]))) for i in idx]
        sq = lambda ms: [mm(m.astype(BF16), blocks(m)) for m in ms]
        p2 = sq(nab)
        p4 = sq(p2)
        eye2 = jnp.where((lane_l == row_l) | (lane_l == row_l + L), 1.0, 0.0)
        tinv = [eye2 + m for m in nab]
        tinv = [t + mm(t.astype(BF16), blocks(q)) for t, q in zip(tinv, p2)]
        tinv = [t + mm(t.astype(BF16), blocks(q)) for t, q in zip(tinv, p4)]
        if L > 8:
            p8 = sq(p4)
            tinv = [t + mm(t.astype(BF16), blocks(q)) for t, q in zip(tinv, p8)]
        sab = [mm(tinv[i].astype(BF16), rows_of(ev(x[i]), od(x[i]))).astype(BF16) for i in idx]
        for i, (n, p) in enumerate(tiles):
            y = x1[i][L:] + mm(nr[i].astype(BF16), rows_of(ev(sab[i]), ev(v[i]), od(sab[i]), od(v[i])))
            y_ref[n, rows, LANES * p:LANES * (p + 1)] = y
        for i, (n, p) in enumerate(tiles):
            upd = lax.dot_general(jnp.concatenate([b[i], k[i]], axis=0), jnp.concatenate([sab[i], v[i]], axis=0),
                                  (((0,), (0,)), ((), ())), preferred_element_type=F32)
            decay = jnp.broadcast_to(wl_all[n][:, LANES * p:LANES * (p + 1)], (LANES, LANES)).T
            s_ref[n, p] = decay * (s0[i] + jnp.where(same_head, upd, 0.0))
        return carry

    lax.fori_loop(0, tc // L, chunk, 0)


def _rwkv_chunk(at, rt, bt, kt, v, wl, tc):
    b, t, c = at.shape
    nb = 4 if b % 4 == 0 else (2 if b % 2 == 0 else 1)
    tok = pl.BlockSpec((nb, tc, c), lambda bi, i: (bi, i, 0))
    return pl.pallas_call(
        _rwkv_chunk_kernel,
        grid=(b // nb, t // tc),
        in_specs=[tok, tok, tok, tok, tok, pl.BlockSpec((nb, tc // REC_CHUNK, c), lambda bi, i: (bi, i, 0))],
        out_specs=tok,
        out_shape=jax.ShapeDtypeStruct((b, t, c), F32),
        scratch_shapes=[pltpu.VMEM((nb, RW_PAIRS, LANES, LANES), F32)],
        compiler_params=_cparams(("parallel", "arbitrary")),
        name="rwkv_chunk",
    )(at, rt, bt, kt, v, wl)


def _rwkv_post_kernel(y_ref, bonus_ref, g_ref, lnw_ref, lnb_ref, e_ref, et_ref, wb_ref, o_ref):
    y = y_ref[...]
    e, et = e_ref[...], et_ref[...]
    mean = _pick_dot(_pick_dot(y, e), et) * (1.0 / RW_HEAD)
    yc = y - mean
    var = _pick_dot(_pick_dot(yc * yc, e), et) * (1.0 / RW_HEAD)
    y = yc * lax.rsqrt(var + RW_GN_EPS) * lnw_ref[...] + lnb_ref[...]
    y = (y + bonus_ref[...]) * g_ref[...]
    o_ref[...] = _bdot(y, wb_ref[...])


def _rwkv_post(y, bonus, g, prm, tt):
    b, t, c = y.shape
    tok = pl.BlockSpec((None, tt, c), lambda bi, i: (bi, i, 0))
    names = ["ln_w", "ln_b", "seg", "seg_t", "w_branch"]
    return pl.pallas_call(
        _rwkv_post_kernel,
        grid=(b, t // tt),
        in_specs=[tok, tok, tok] + [_full(prm[nm].shape) for nm in names],
        out_specs=pl.BlockSpec((None, tt, D_MODEL), lambda bi, i: (bi, i, 0)),
        out_shape=jax.ShapeDtypeStruct((b, t, D_MODEL), F32),
        compiler_params=_cparams(("parallel", "parallel")),
        name="rwkv_post",
    )(y, bonus, g, *[prm[nm] for nm in names])


def _ssd_kernel(z_ref, xbc_ref, dt_ref, cw_ref, cb_ref, dtb_ref, alog_ref, dskip_ref, ng_ref, ex_ref, tri_ref, wb_ref,
                o_ref, buf, st_ref):
    L = SSD_CHUNK
    first = pl.program_id(1) == 0

    @pl.when(first)
    def _():
        buf[0:SUBLANES, :] = jnp.zeros((SUBLANES, SSD_CONV_CH), F32)
        st_ref[...] = jnp.zeros_like(st_ref)

    buf[SUBLANES:, :] = xbc_ref[...]
    window = buf[...]
    conv = cb_ref[...] + window[SUBLANES:, :] * cw_ref[SSD_CONV - 1:SSD_CONV, :]
    for j in range(SSD_CONV - 1):
        lag = SSD_CONV - 1 - j
        conv = conv + pltpu.roll(window, lag, 0)[SUBLANES:, :] * cw_ref[j:j + 1, :]
    buf[0:SUBLANES, :] = window[L:, :]
    xa = conv * _sigmoid(conv)
    xs = xa[:, :SSD_WIDTH]

    dt = _softplus(dt_ref[...] + dtb_ref[...])
    a = dt * (-jnp.exp(alog_ref[...]))
    row = lax.broadcasted_iota(jnp.int32, (L, L), 0)
    col = lax.broadcasted_iota(jnp.int32, (L, L), 1)
    causal = row >= col
    acs = _pick_dot_left(tri_ref[...], a)
    acs_last = acs[L - 1:L, :]
    stacked = jnp.concatenate(
        [dt, jnp.exp(acs), jnp.exp(acs_last - acs), jnp.broadcast_to(jnp.exp(acs_last), (SUBLANES, LANES))], axis=0)
    wide = _pick_dot(stacked, ex_ref[...])
    dt_x, ea_x, dte_x, el_x = wide[:L], wide[L:2 * L], wide[2 * L:3 * L], wide[3 * L:3 * L + 1]
    acs_t = acs.T
    xdt = xs * dt_x
    lane = lax.broadcasted_iota(jnp.int32, (L, LANES), 1)
    is_lo = lane < SSD_HEAD

    y_parts = []
    for g in range(SSD_GROUPS):
        bg = xa[:, SSD_WIDTH + SSD_STATE * g:SSD_WIDTH + SSD_STATE * (g + 1)]
        cg = xa[:, SSD_WIDTH + SSD_STATE * (SSD_GROUPS + g):SSD_WIDTH + SSD_STATE * (SSD_GROUPS + g + 1)]
        cb = lax.dot_general(cg.astype(BF16), bg.astype(BF16), (((1,), (1,)), ((), ())), preferred_element_type=F32)
        gl = slice(SSD_GW * g, SSD_GW * (g + 1))
        y_off = _bdot(cg, st_ref[g]) * ea_x[:, gl]
        diag = []
        for pp in range(SSD_HG // 2):
            acc = None
            xp = xdt[:, SSD_GW * g + LANES * pp:SSD_GW * g + LANES * (pp + 1)]
            for half in range(2):
                h = SSD_HG * g + 2 * pp + half
                seg = acs[:, h:h + 1] - acs_t[h:h + 1, :]
                lm = jnp.exp(jnp.where(causal, seg, -BIG))
                xh = jnp.where(is_lo if half == 0 else jnp.logical_not(is_lo), xp, 0.0)
                term = _bdot(lm * cb, xh)
                acc = term if acc is None else acc + term
            diag.append(acc)
        y_parts.append(jnp.concatenate(diag, axis=1) + y_off)
        st_ref[g] = st_ref[g] * el_x[:, gl] + _bdot(bg.T, xdt[:, gl] * dte_x[:, gl])
    y = jnp.concatenate(y_parts, axis=1)
    y = y + xs * dskip_ref[...]
    z = z_ref[...]
    y = y * (z * _sigmoid(z))
    normed = []
    for g in range(SSD_GROUPS):
        yg = y[:, SSD_GW * g:SSD_GW * (g + 1)]
        ms = jnp.mean(yg * yg, axis=-1, keepdims=True)
        normed.append(yg * lax.rsqrt(ms + NORM_EPS))
    y = jnp.concatenate(normed, axis=1) * ng_ref[...]
    o_ref[...] = _bdot(y, wb_ref[...])


def _ssd(proj3, prm):
    b, t, _ = proj3.shape
    L = SSD_CHUNK
    tok = lambda w, off: pl.BlockSpec((None, L, w), lambda bi, i: (bi, i, off // w))
    names = ["conv_w", "conv_b", "dt_bias", "a_log", "d_skip", "norm_g", "expand", "tri", "w_branch"]
    return pl.pallas_call(
        _ssd_kernel,
        grid=(b, t // L),
        in_specs=[tok(W_Z, OFF_Z), tok(W_XBC, OFF_XBC), tok(W_DT, OFF_DT)] + [_full(prm[nm].shape) for nm in names],
        out_specs=pl.BlockSpec((None, L, D_MODEL), lambda bi, i: (bi, i, 0)),
        out_shape=jax.ShapeDtypeStruct((b, t, D_MODEL), F32),
        scratch_shapes=[pltpu.VMEM((L + SUBLANES, SSD_CONV_CH), F32), pltpu.VMEM((SSD_GROUPS, SSD_STATE, SSD_GW), F32)],
        compiler_params=_cparams(("parallel", "arbitrary")),
        name="ssd",
    )(proj3, proj3, proj3, *[prm[nm] for nm in names])


def _mix_kernel(ya_ref, yb_ref, gate_ref, h_ref, wo_ref, gf_ref, h_o, ut_o):
    gt = gate_ref[...]
    mixed = _sigmoid(gt[:, :D_MODEL]) * ya_ref[...] + _sigmoid(gt[:, D_MODEL:]) * yb_ref[...]
    h = h_ref[...] + _bdot(mixed, wo_ref[...])
    h_o[...] = h
    ms = jnp.mean(h * h, axis=-1, keepdims=True)
    u = h * lax.rsqrt(ms + NORM_EPS) * gf_ref[...]
    ut_o[...] = pltpu.bitcast(u.T.astype(BF16), jnp.uint32)


def _mix(ya, yb, proj, h, prm, tm):
    n = h.shape[0]
    names = ["w_out", "norm_ffn_g"]
    tok = lambda w, c: pl.BlockSpec((tm, w), lambda i: (i, c))
    return pl.pallas_call(
        _mix_kernel,
        grid=(n // tm,),
        in_specs=[tok(D_MODEL, 0), tok(D_MODEL, 0), tok(W_GATE, OFF_GATE // W_GATE), tok(D_MODEL, 0)]
        + [_full(prm[nm].shape) for nm in names],
        out_specs=[tok(D_MODEL, 0), pl.BlockSpec((D_MODEL // 2, tm), lambda i: (0, i))],
        out_shape=[jax.ShapeDtypeStruct((n, D_MODEL), F32), jax.ShapeDtypeStruct((D_MODEL // 2, n), jnp.uint32)],
        compiler_params=_cparams(("parallel",)),
        name="mix",
    )(ya, yb, proj, h, *[prm[nm] for nm in names])


def _top_rows(x, n, want_rank):
    tm = x.shape[1]
    row_n = lax.broadcasted_iota(jnp.int32, (n, tm), 0)
    tops = jnp.zeros((n, tm), F32)
    rank = jnp.full(x.shape, float(n), F32) if want_rank else None
    m = None
    for k in range(n):
        m = jnp.max(x, axis=0, keepdims=True)
        tops = jnp.where(row_n == k, m, tops)
        hit = x >= m
        if want_rank:
            rank = jnp.where(hit, float(k), rank)
        if k + 1 < n:
            x = jnp.where(hit, -BIG, x)
    return tops, m, rank


def _peer_sel_kernel(ut_ref, wq_ref, keys_ref, e1_o, cnt_o, e2_o, r2_o):
    k = PEER_TOPK
    qt = jnp.dot(wq_ref[...], pltpu.bitcast(ut_ref[...], BF16), preferred_element_type=F32)
    for h in range(PEER_HEADS):
        s = []
        for c in range(2):
            j = 2 * h + c
            s.append(_bdot(keys_ref[j], qt[LANES * j:LANES * (j + 1), :]))
        s1, s2 = s
        top1, thr1, _ = _top_rows(s1, k, False)
        top2, _, rank2 = _top_rows(s2, k, True)
        in1 = s1 >= thr1
        in2 = rank2 < float(k)
        row8 = lax.broadcasted_iota(jnp.int32, (SUBLANES, s1.shape[1]), 0)
        groups = [top1[0:1, :] + top2]
        for k1 in range(1, SUBLANES):
            width = k // (k1 + 1)
            groups.append(jnp.where(row8 < width, top1[k1:k1 + 1, :] + top2[:SUBLANES, :], -BIG))
        groups.append(top1[SUBLANES:, :] + top2[0:1, :])
        x = jnp.concatenate(groups, axis=0)
        m0 = zsum = m = None
        for i in range(k):
            m = jnp.max(x, axis=0, keepdims=True)
            if i == 0:
                m0, zsum = m, jnp.ones_like(m)
            else:
                zsum = zsum + jnp.exp(m - m0)
            if i + 1 < k:
                x = jnp.where(x >= m, -BIG, x)
        thr = m
        cnt_rank = jnp.zeros_like(top1)
        for k2 in range(k):
            cnt_rank = cnt_rank + jnp.where(top1 + top2[k2:k2 + 1, :] >= thr, 1.0, 0.0)
        cnt = jnp.zeros_like(s1)
        for k1 in range(k):
            cnt = jnp.where(s1 == top1[k1:k1 + 1, :], cnt_rank[k1:k1 + 1, :], cnt)
        e1_o[h] = jnp.where(in1, jnp.exp(s1 - top1[0:1, :]), 0.0) / zsum
        cnt_o[h] = cnt
        e2_o[h] = pltpu.bitcast(jnp.where(in2, jnp.exp(s2 - top2[0:1, :]), 0.0).astype(BF16), jnp.uint32)
        r2_o[h] = pltpu.bitcast(rank2.astype(BF16), jnp.uint32)


def _peer_sel(ut, prm, tm):
    n = ut.shape[1]
    shp = (PEER_HEADS, PEER_NKEYS, n)
    ospec = pl.BlockSpec((PEER_HEADS, PEER_NKEYS, tm), lambda i: (0, 0, i))
    pshp = (PEER_HEADS, PEER_NKEYS // 2, n)
    pspec = pl.BlockSpec((PEER_HEADS, PEER_NKEYS // 2, tm), lambda i: (0, 0, i))
    return pl.pallas_call(
        _peer_sel_kernel,
        grid=(n // tm,),
        in_specs=[pl.BlockSpec((D_MODEL // 2, tm), lambda i: (0, i)), _full(prm["wq_t"].shape), _full(prm["keys"].shape)],
        out_specs=[ospec, ospec, pspec, pspec],
        out_shape=[jax.ShapeDtypeStruct(shp, F32), jax.ShapeDtypeStruct(shp, F32),
                   jax.ShapeDtypeStruct(pshp, jnp.uint32), jax.ShapeDtypeStruct(pshp, jnp.uint32)],
        compiler_params=_cparams(("parallel",)),
        name="peer_sel",
    )(ut, prm["wq_t"], prm["keys"])


def _gated_gelu(x, gate):
    c = 0.7978845608028654
    t = jnp.tanh(x * (c + (c * 0.044715) * (x * x)))
    hx = x.astype(BF16) * 0.5
    return (hx + hx * t.astype(BF16)) * gate


def _peer_ffn_kernel(final_g, ut_ref, down_ref, upt_ref, uptp_ref, e1_ref, cnt_ref, e2_ref, r2_ref, h_ref, *rest):
    if final_g:
        gfin_ref, o_ref, *scr = rest
    else:
        o_ref, *scr = rest
    nh = len(scr) // 5
    acc, act, gbuf, hs, tok = (scr[nh * q:nh * (q + 1)] for q in range(5))
    j = pl.program_id(1)
    te, th = act[0].shape
    sub = 2 * SUBLANES

    @pl.when(j == 0)
    def _():
        for hf in range(nh):
            acc[hf][...] = jnp.zeros_like(acc[hf])
            tok[hf][...] = pltpu.bitcast(ut_ref[:, th * hf:th * (hf + 1)], BF16)
        hs[nh - 1][...] = jnp.zeros_like(hs[nh - 1])

    kb = 4
    blk = kb * PEER_NKEYS
    nblk = te // blk
    nm = D_MODEL // blk
    sbt = 4

    def project_down(hf, b):
        rows = slice(blk * b, blk * (b + 1))
        lhs = pltpu.bitcast(down_ref[blk // 2 * b:blk // 2 * (b + 1), :], BF16)
        act[hf][rows, :] = jnp.dot(lhs, tok[hf][...], preferred_element_type=F32)

    def build_gates(hf, b):
        for c in range(th // LANES):
            cl = slice(th * hf + LANES * c, th * hf + LANES * (c + 1))
            first = lambda ref, h, q: jnp.broadcast_to(ref[h, kb * b + q:kb * b + q + 1, cl], (sub, LANES)).astype(BF16)
            cnt = [[first(cnt_ref, h, q) for q in range(kb)] for h in range(PEER_HEADS)]
            e1 = [[first(e1_ref, h, q) for q in range(kb)] for h in range(PEER_HEADS)]
            for s0 in range(0, PEER_NKEYS // sub, sbt):
                gate = [[jnp.zeros((sub, LANES), BF16) for _ in range(sbt)] for _ in range(kb)]
                for h in range(PEER_HEADS):
                    r2 = [pltpu.bitcast(r2_ref[h, SUBLANES * (s0 + s):SUBLANES * (s0 + s + 1), cl], BF16) for s in range(sbt)]
                    e2 = [pltpu.bitcast(e2_ref[h, SUBLANES * (s0 + s):SUBLANES * (s0 + s + 1), cl], BF16) for s in range(sbt)]
                    for q in range(kb):
                        for s in range(sbt):
                            gate[q][s] = gate[q][s] + jnp.where(r2[s] < cnt[h][q], e2[s], 0.0) * e1[h][q]
                for q in range(kb):
                    for s in range(sbt):
                        gw = (blk * b + PEER_NKEYS * q + sub * (s0 + s)) // 2
                        gbuf[hf][gw:gw + SUBLANES, LANES * c:LANES * (c + 1)] = pltpu.bitcast(gate[q][s], jnp.uint32)

    def activate(hf, b):
        for r0 in range(blk * b, blk * (b + 1), sub):
            for c in range(th // LANES):
                cl = slice(LANES * c, LANES * (c + 1))
                gate = pltpu.bitcast(gbuf[hf][r0 // 2:r0 // 2 + SUBLANES, cl], BF16)
                hs[hf][r0:r0 + sub, cl] = _gated_gelu(act[hf][r0:r0 + sub, cl], gate)

    def project_up(w_ref, hf, m):
        rows = slice(blk * m, blk * (m + 1))
        lhs = pltpu.bitcast(w_ref[blk // 2 * m:blk // 2 * (m + 1), :], BF16)
        acc[hf][rows, :] += jnp.dot(lhs, hs[hf][...], preferred_element_type=F32)

    for hf in range(nh):
        for b in range(nblk):
            project_down(hf, b)
            for m in range(b * nm // nblk, (b + 1) * nm // nblk):
                if hf == 0:
                    project_up(uptp_ref, nh - 1, m)
                else:
                    project_up(upt_ref, hf - 1, m)
            build_gates(hf, b)
            if b > 0:
                activate(hf, b - 1)
        activate(hf, nblk - 1)

    @pl.when(j == pl.num_programs(1) - 1)
    def _():
        for m in range(nm):
            project_up(upt_ref, nh - 1, m)
        for hf in range(nh):
            h = h_ref[th * hf:th * (hf + 1), :] + acc[hf][...].T
            if final_g:
                ms = jnp.mean(h * h, axis=-1, keepdims=True)
                h = h * lax.rsqrt(ms + NORM_EPS) * gfin_ref[...]
            o_ref[th * hf:th * (hf + 1), :] = h


def _peer_ffn(ut, down, upt, e1, cnt, e2, r2, h, final_g, tl, te):
    n = h.shape[0]
    nh = 2 if tl % (2 * LANES) == 0 else 1
    sel = pl.BlockSpec((PEER_HEADS, PEER_NKEYS // 2, tl), lambda i, j: (0, 0, i))
    sel1 = pl.BlockSpec((PEER_HEADS, te // PEER_NKEYS, tl), lambda i, j: (0, j, i))
    in_specs = [
        pl.BlockSpec((D_MODEL // 2, tl), lambda i, j: (0, i)),
        pl.BlockSpec((te // 2, D_MODEL), lambda i, j: (j, 0)),
        pl.BlockSpec((D_MODEL // 2, te), lambda i, j: (0, j)),
        pl.BlockSpec((D_MODEL // 2, te), lambda i, j: (0, jnp.maximum(j - 1, 0))),
        sel1, sel1, sel, sel,
        pl.BlockSpec((tl, D_MODEL), lambda i, j: (i, 0)),
    ]
    args = [ut, down, upt, upt, e1, cnt, e2, r2, h]
    if final_g is not None:
        in_specs.append(_full(final_g.shape))
        args.append(final_g)
    return pl.pallas_call(
        functools.partial(_peer_ffn_kernel, final_g is not None),
        grid=(n // tl, PEER_EXPERTS // te),
        in_specs=in_specs,
        out_specs=pl.BlockSpec((tl, D_MODEL), lambda i, j: (i, 0)),
        out_shape=jax.ShapeDtypeStruct((n, D_MODEL), F32),
        scratch_shapes=([pltpu.VMEM((D_MODEL, tl // nh), F32)] * nh + [pltpu.VMEM((te, tl // nh), F32)] * nh
                        + [pltpu.VMEM((te // 2, tl // nh), jnp.uint32)] * nh + [pltpu.VMEM((te, tl // nh), BF16)] * nh
                        + [pltpu.VMEM((D_MODEL, tl // nh), BF16)] * nh),
        compiler_params=_cparams(("parallel", "arbitrary")),
        name="peer_ffn",
    )(*args)


def _pack_weight_kernel(transpose, w_ref, o_ref):
    w = w_ref[...]
    if transpose:
        w = w.T
    o_ref[...] = pltpu.bitcast(w.astype(BF16), jnp.uint32)


def _pack_weight(w_layers, layer, transpose):
    _, r, c = w_layers.shape
    t = D_MODEL
    out_shape, out_map = ((c // 2, r), lambda i, j: (j, i)) if transpose else ((r // 2, c), lambda i, j: (i, j))
    return pl.pallas_call(
        functools.partial(_pack_weight_kernel, transpose),
        grid=(r // t, c // t),
        in_specs=[pl.BlockSpec((None, t, t), lambda i, j: (layer, i, j))],
        out_specs=pl.BlockSpec((t // 2, t), out_map),
        out_shape=jax.ShapeDtypeStruct(out_shape, jnp.uint32),
        compiler_params=_cparams(("parallel", "parallel")),
        name="pack_weight",
    )(w_layers)


def _pad_rows(w, start, total):
    return jnp.zeros((total, w.shape[1]), w.dtype).at[start:start + w.shape[0]].set(w)


def _pad_lanes(v, total):
    return jnp.zeros((1, total), v.dtype).at[0, :v.shape[0]].set(v)


def _row(v):
    return v.reshape(1, -1)


def _constants():
    c = jnp.arange(RW_WIDTH)[:, None] // RW_HEAD == jnp.arange(LANES)[None, :]
    seg = c.astype(BF16)
    expand = (jnp.arange(LANES)[:, None] == jnp.arange(SSD_WIDTH)[None, :] // SSD_HEAD).astype(BF16)
    return seg, seg.T, expand


def kernel(x, norm_mix_g, w_in, rwkv_mu, rwkv_w0, rwkv_w2, rwkv_a0, rwkv_a2, rwkv_g2, rwkv_v0, rwkv_v1, rwkv_v2,
           rwkv_k_k, rwkv_k_a, rwkv_r_k, rwkv_ln_w, rwkv_ln_b, w_rwkv_branch, ssd_conv_w, ssd_conv_b, ssd_dt_bias,
           ssd_A_log, ssd_D, ssd_norm_g, w_ssd_branch, w_out, norm_ffn_g, peer_w_q, peer_sub_keys, peer_down, peer_up,
           norm_final_g):
    bsz, t, d = x.shape
    n = bsz * t
    depth = w_in.shape[0]
    seg, seg_t, expand = _constants()
    h = x.reshape(n, d)
    v_first = None
    o_lora = 3 * RW_WIDTH
    o_z = o_lora + RW_LORA
    o_xbc = o_z + SSD_WIDTH
    o_dt = o_xbc + SSD_CONV_CH
    o_gate = o_dt + SSD_HEADS
    tt = min(TOK_TILE, t)
    step = jnp.arange(tt)
    chunk_tri = ((step[:, None] // REC_CHUNK == step[None, :] // REC_CHUNK) & (step[:, None] >= step[None, :])).astype(BF16)
    chunk_last = (jnp.arange(tt // REC_CHUNK)[:, None] * REC_CHUNK + REC_CHUNK - 1 == step[None, :]).astype(BF16)
    ssd_tri = (jnp.arange(SSD_CHUNK)[:, None] >= jnp.arange(SSD_CHUNK)[None, :]).astype(BF16)
    for l in range(depth):
        wl = w_in[l]
        zpad = lambda k: jnp.zeros((d, k), wl.dtype)
        w_cat = jnp.concatenate([
            wl[:, :o_lora], wl[:, o_xbc:o_dt], wl[:, o_z:o_xbc], wl[:, o_gate:],
            wl[:, o_dt:o_gate], zpad(LANES - SSD_HEADS), wl[:, o_lora:o_z], zpad(RW_LORA_PAD - RW_LORA)], axis=1).astype(BF16)
        proj = _norm_matmul(h, _row(norm_mix_g[l]), w_cat, tm=min(PROJ_TILE[0], n), tn=PROJ_TILE[1])
        proj3 = proj.reshape(bsz, t, PROJ_PAD)

        mu = rwkv_mu[l]
        rw = {
            "mu1": _row(mu[:o_lora]), "mu2": _pad_lanes(mu[o_lora:], RW_LORA_PAD),
            "w0": _row(rwkv_w0[l]), "w2": _pad_rows(rwkv_w2[l], 0, RW_LORA_PAD).astype(BF16),
            "a0": _row(rwkv_a0[l]), "a2": _pad_rows(rwkv_a2[l], DECAY_LORA, RW_LORA_PAD).astype(BF16),
            "g2": _pad_rows(rwkv_g2[l], DECAY_LORA + ICLR_LORA, RW_LORA_PAD).astype(BF16),
            "k_k": _row(rwkv_k_k[l]), "k_a": _row(rwkv_k_a[l]), "r_k": _row(rwkv_r_k[l]),
            "seg": seg, "seg_t": seg_t, "chunk_tri": chunk_tri, "chunk_last": chunk_last,
            "ln_w": _row(rwkv_ln_w[l]), "ln_b": _row(rwkv_ln_b[l]), "w_branch": w_rwkv_branch[l].astype(BF16),
        }
        if l > 0:
            rw["v0"] = _row(rwkv_v0[l - 1])
            rw["v1"] = jnp.zeros((RW_WIDTH, VRES_PAD), F32).at[:, :VRES_LORA].set(rwkv_v1[l - 1]).astype(BF16)
            rw["v2"] = _pad_rows(rwkv_v2[l - 1], 0, VRES_PAD).astype(BF16)
        outs = _rwkv_prep(proj3, v_first, rw, tt)
        at, rt, bt, kt, vb, w_end, g_, bonus = outs[:8]
        if l == 0:
            v_first = outs[8]
        y_r = _rwkv_chunk(at, rt, bt, kt, vb, w_end, tc=tt)
        y_a = _rwkv_post(y_r, bonus, g_, rw, tt)

        sp = {
            "conv_w": ssd_conv_w[l], "conv_b": _row(ssd_conv_b[l]),
            "dt_bias": _pad_lanes(ssd_dt_bias[l], LANES), "a_log": _pad_lanes(ssd_A_log[l], LANES),
            "d_skip": _row(jnp.repeat(ssd_D[l], SSD_HEAD)), "norm_g": _row(ssd_norm_g[l]),
            "expand": expand, "tri": ssd_tri, "w_branch": w_ssd_branch[l].astype(BF16),
        }
        y_b = _ssd(proj3, sp)

        pp = {
            "w_out": w_out[l].astype(BF16), "norm_ffn_g": _row(norm_ffn_g[l]), "wq_t": peer_w_q[l].T.astype(BF16),
            "keys": peer_sub_keys[l].reshape(2 * PEER_HEADS, PEER_NKEYS, PEER_QDIM // 2).astype(BF16),
        }
        h, ut = _mix(y_a.reshape(n, d), y_b.reshape(n, d), proj, h, pp, tm=min(TOK_TILE, n))
        e1, cnt, e2, r2 = _peer_sel(ut, pp, tm=min(TOK_TILE, n))
        fin = _row(norm_final_g) if l == depth - 1 else None
        h = _peer_ffn(ut, _pack_weight(peer_down, l, False), _pack_weight(peer_up, l, True), e1, cnt, e2, r2, h, fin,
                      tl=min(FFN_TILE[0], n), te=FFN_TILE[1])
    return h.reshape(bsz, t, d)
```

```python
import functools

import jax
import jax.numpy as jnp
from jax import lax
from jax.experimental import pallas as pl
from jax.experimental.pallas import tpu as pltpu

F32 = jnp.float32
BF16 = jnp.bfloat16

LANES = 128
SUBLANES = 8
NORM_EPS = 1e-6

D_MODEL = 1024
RW_HEAD = 64
RW_WIDTH = D_MODEL
RW_HEADS = RW_WIDTH // RW_HEAD
RW_PAIRS = RW_WIDTH // LANES
DECAY_LORA, ICLR_LORA, GATE_LORA, VRES_LORA = 64, 64, 160, 32
RW_LORA = DECAY_LORA + ICLR_LORA + GATE_LORA
RW_LORA_PAD = 384
VRES_PAD = 128
RW_GN_EPS = 64e-5
REC_CHUNK = 16
SSD_WIDTH = 2 * D_MODEL
SSD_HEAD = 64
SSD_HEADS = SSD_WIDTH // SSD_HEAD
SSD_STATE = 128
SSD_GROUPS = 4
SSD_GW = SSD_WIDTH // SSD_GROUPS
SSD_HG = SSD_HEADS // SSD_GROUPS
SSD_CONV = 4
SSD_CONV_CH = SSD_WIDTH + 2 * SSD_GROUPS * SSD_STATE
SSD_CHUNK = 128
PEER_HEADS = 8
PEER_NKEYS = 128
PEER_TOPK = 16
PEER_QDIM = 256
PEER_EXPERTS = PEER_NKEYS * PEER_NKEYS
BIG = 1e30

OFF_RKV, W_RKV = 0, 3 * RW_WIDTH
OFF_XBC, W_XBC = 3072, SSD_CONV_CH
OFF_Z, W_Z = 6144, SSD_WIDTH
OFF_GATE, W_GATE = 8192, 2 * D_MODEL
OFF_DT, W_DT = 10240, LANES
OFF_LORA, W_LORA = 10368, RW_LORA_PAD
PROJ_PAD = 10752

VMEM_LIMIT = 56 * 1024 * 1024
TOK_TILE = 256
PROJ_TILE = (1024, 1792)
FFN_TILE = (512, 8 * PEER_NKEYS)


def _cparams(sem):
    return pltpu.CompilerParams(dimension_semantics=sem, vmem_limit_bytes=VMEM_LIMIT)


def _full(shape):
    n = len(shape)
    return pl.BlockSpec(shape, lambda *_: (0,) * n)


def _sigmoid(x):
    return 1.0 / (1.0 + jnp.exp(-x))


def _softplus(x):
    return jnp.maximum(x, 0.0) + jnp.log(1.0 + jnp.exp(-jnp.abs(x)))


def _bdot(a, b):
    return jnp.dot(a.astype(BF16), b.astype(BF16), preferred_element_type=F32)


def _split(x):
    hi = x.astype(BF16)
    return hi, (x - hi.astype(F32)).astype(BF16)


def _pick_dot(x, sel):
    hi, lo = _split(x)
    return jnp.dot(hi, sel, preferred_element_type=F32) + jnp.dot(lo, sel, preferred_element_type=F32)


def _pick_dot_left(sel, x):
    hi, lo = _split(x)
    return jnp.dot(sel, hi, preferred_element_type=F32) + jnp.dot(sel, lo, preferred_element_type=F32)


def _norm_matmul_kernel(x_ref, g_ref, w_ref, o_ref, u_ref):
    @pl.when(pl.program_id(1) == 0)
    def _():
        x = x_ref[...]
        ms = jnp.mean(x * x, axis=-1, keepdims=True)
        u_ref[...] = (x * lax.rsqrt(ms + NORM_EPS) * g_ref[...]).astype(BF16)

    o_ref[...] = jnp.dot(u_ref[...], w_ref[...], preferred_element_type=F32)


def _norm_matmul(x, g, w, tm, tn):
    n, d = x.shape
    m = w.shape[1]
    return pl.pallas_call(
        _norm_matmul_kernel,
        grid=(n // tm, m // tn),
        in_specs=[
            pl.BlockSpec((tm, d), lambda i, j: (i, 0)),
            pl.BlockSpec((1, d), lambda i, j: (0, 0)),
            pl.BlockSpec((d, tn), lambda i, j: (0, j)),
        ],
        out_specs=pl.BlockSpec((tm, tn), lambda i, j: (i, j)),
        out_shape=jax.ShapeDtypeStruct((n, m), F32),
        scratch_shapes=[pltpu.VMEM((tm, d), BF16)],
        compiler_params=_cparams(("parallel", "arbitrary")),
        name="in_proj",
    )(x, g, w)


def _shift_mix(x, prev_ref, mu):
    rows = lax.broadcasted_iota(jnp.int32, x.shape, 0)
    xs = jnp.where(rows == 0, prev_ref[...], pltpu.roll(x, 1, 0))
    prev_ref[...] = x[x.shape[0] - 1:, :]
    return x + (xs - x) * mu


def _rwkv_prep_kernel(has_vres, *refs):
    if has_vres:
        (prkv_ref, plora_ref, vf_ref, mu1_ref, mu2_ref, w0_ref, w2_ref, a0_ref, a2_ref, g2_ref,
         kk_ref, ka_ref, rk_ref, e_ref, et_ref, ctri_ref, csel_ref, v0_ref, v1_ref, v2_ref,
         at_o, rt_o, bt_o, kt_o, vb_o, wl_o, g_o, bonus_o, prev1, prev2) = refs
    else:
        (prkv_ref, plora_ref, mu1_ref, mu2_ref, w0_ref, w2_ref, a0_ref, a2_ref, g2_ref,
         kk_ref, ka_ref, rk_ref, e_ref, et_ref, ctri_ref, csel_ref,
         at_o, rt_o, bt_o, kt_o, vb_o, wl_o, g_o, bonus_o, v_o, prev1, prev2) = refs
    @pl.when(pl.program_id(1) == 0)
    def _():
        prev1[...] = jnp.zeros_like(prev1)
        prev2[...] = jnp.zeros_like(prev2)

    p = _shift_mix(prkv_ref[...], prev1, mu1_ref[...])
    lo = _shift_mix(plora_ref[...], prev2, mu2_ref[...])
    r = p[:, :RW_WIDTH]
    k = p[:, RW_WIDTH:2 * RW_WIDTH]
    v = p[:, 2 * RW_WIDTH:]
    w_log = -_softplus(-(w0_ref[...] + _bdot(jnp.tanh(lo), w2_ref[...]))) - 0.5
    log_w = -jnp.exp(w_log)
    a = _sigmoid(a0_ref[...] + _bdot(lo, a2_ref[...]))
    g = _bdot(_sigmoid(lo), g2_ref[...])
    if has_vres:
        mix = _sigmoid(v0_ref[...] + _bdot(_bdot(v, v1_ref[...]), v2_ref[...]))
        v = v + (vf_ref[...] - v) * mix
    else:
        v_o[...] = v
    e, et = e_ref[...], et_ref[...]
    kk = k * kk_ref[...]
    inv = lax.rsqrt(_pick_dot(kk * kk, e) + 1e-12)
    kk = kk * _pick_dot(inv, et)
    k2 = k * (1.0 + (a - 1.0) * ka_ref[...])
    rk = _pick_dot(_pick_dot(r * k2 * rk_ref[...], e), et)
    cum = _pick_dot_left(ctri_ref[...], log_w)
    grow, shrink = jnp.exp(cum), jnp.exp(-cum)
    at_o[...] = (-kk * jnp.exp(cum - log_w)).astype(BF16)
    rt_o[...] = (r * grow).astype(BF16)
    bt_o[...] = (kk * a * shrink).astype(BF16)
    kt_o[...] = (k2 * shrink).astype(BF16)
    vb_o[...] = v.astype(BF16)
    wl_o[...] = jnp.exp(_pick_dot_left(csel_ref[...], cum))
    g_o[...] = g
    bonus_o[...] = rk * v


def _rwkv_prep(proj3, vfirst, prm, tt):
    b, t, _ = proj3.shape
    has_vres = vfirst is not None
    tok = lambda w, c: pl.BlockSpec((None, tt, w), lambda bi, i: (bi, i, c))
    in_specs = [tok(W_RKV, OFF_RKV // W_RKV), tok(W_LORA, OFF_LORA // W_LORA)]
    args = [proj3, proj3]
    if has_vres:
        in_specs.append(tok(RW_WIDTH, 0))
        args.append(vfirst)
    names = ["mu1", "mu2", "w0", "w2", "a0", "a2", "g2", "k_k", "k_a", "r_k", "seg", "seg_t", "chunk_tri", "chunk_last"]
    if has_vres:
        names += ["v0", "v1", "v2"]
    for nm in names:
        in_specs.append(_full(prm[nm].shape))
        args.append(prm[nm])
    out_shape = [jax.ShapeDtypeStruct((b, t, RW_WIDTH), BF16)] * 5
    out_specs = [tok(RW_WIDTH, 0)] * 5
    out_shape.append(jax.ShapeDtypeStruct((b, t // REC_CHUNK, RW_WIDTH), F32))
    out_specs.append(pl.BlockSpec((None, tt // REC_CHUNK, RW_WIDTH), lambda bi, i: (bi, i, 0)))
    out_shape += [jax.ShapeDtypeStruct((b, t, RW_WIDTH), F32)] * 2
    out_specs += [tok(RW_WIDTH, 0)] * 2
    if not has_vres:
        out_shape.append(jax.ShapeDtypeStruct((b, t, RW_WIDTH), F32))
        out_specs.append(tok(RW_WIDTH, 0))
    return pl.pallas_call(
        functools.partial(_rwkv_prep_kernel, has_vres),
        grid=(b, t // tt),
        in_specs=in_specs,
        out_specs=out_specs,
        out_shape=out_shape,
        scratch_shapes=[pltpu.VMEM((1, W_RKV), F32), pltpu.VMEM((1, W_LORA), F32)],
        compiler_params=_cparams(("parallel", "arbitrary")),
        name="rwkv_prep",
    )(*args)


def _rwkv_chunk_kernel(at_ref, rt_ref, bt_ref, kt_ref, v_ref, wl_ref, y_ref, s_ref):
    nb, tc = at_ref.shape[0], at_ref.shape[1]
    L = REC_CHUNK
    pairs = range(RW_PAIRS)

    @pl.when(pl.program_id(1) == 0)
    def _():
        s_ref[...] = jnp.zeros_like(s_ref)

    lane_l = lax.broadcasted_iota(jnp.int32, (L, LANES), 1)
    row_l = lax.broadcasted_iota(jnp.int32, (L, LANES), 0)
    is_e = lane_l < RW_HEAD
    row_p = lax.broadcasted_iota(jnp.int32, (4 * L, LANES), 0)
    lane_p = lax.broadcasted_iota(jnp.int32, (4 * L, LANES), 1)
    t_p, j_p = row_p % L, lane_p % L
    keep = (lane_p < 2 * L) & ((j_p < t_p) | (((row_p // L) % 2 == 1) & (j_p == t_p)))
    row_s = lax.broadcasted_iota(jnp.int32, (LANES, LANES), 0)
    lane_s = lax.broadcasted_iota(jnp.int32, (LANES, LANES), 1)
    same_head = (row_s < RW_HEAD) == (lane_s < RW_HEAD)
    zeros_b = lambda r: jnp.zeros((r, LANES), BF16)
    mm = lambda a, b: jnp.dot(a, b, preferred_element_type=F32)

    def chunk(j, carry):
        rows = pl.ds(pl.multiple_of(j * L, L), L)
        grp = pl.ds(pl.multiple_of((j // SUBLANES) * SUBLANES, SUBLANES), SUBLANES)
        pick = lax.broadcasted_iota(jnp.int32, (SUBLANES, RW_WIDTH), 0) == j % SUBLANES
        tiles = [(n, p) for n in range(nb) for p in pairs]
        wl_all = [jnp.sum(jnp.where(pick, wl_ref[n, grp, :], 0.0), axis=0, keepdims=True) for n in range(nb)]
        ld = lambda ref: [ref[n, rows, LANES * p:LANES * (p + 1)] for n, p in tiles]
        a, r, b, k, v = ld(at_ref), ld(rt_ref), ld(bt_ref), ld(kt_ref), ld(v_ref)
        ev = lambda m: jnp.where(is_e, m, jnp.zeros_like(m))
        od = lambda m: jnp.where(is_e, jnp.zeros_like(m), m)
        idx = range(len(tiles))
        sc = []
        for i in idx:
            lhs = jnp.concatenate([ev(a[i]), ev(r[i]), od(a[i]), od(r[i])], axis=0)
            rhs = jnp.concatenate([b[i], k[i], zeros_b(LANES - 2 * L)], axis=0)
            s_i = lax.dot_general(lhs, rhs, (((1,), (1,)), ((), ())), preferred_element_type=F32)
            sc.append(jnp.where(keep, s_i, 0.0))
        s0 = [s_ref[n, p] for n, p in tiles]
        x1 = [mm(jnp.concatenate([a[i], r[i]], axis=0), s0[i].astype(BF16)) for i in idx]
        in_e, in_o = lane_l < L, (lane_l >= L) & (lane_l < 2 * L)
        side = lambda m_e, m_o: jnp.where(in_e, m_e, 0.0) + jnp.where(in_o, m_o, 0.0)
        nab = [side(sc[i][:L], pltpu.roll(sc[i][2 * L:3 * L], L, 1)) for i in idx]
        nak = [side(pltpu.roll(sc[i][:L], LANES - L, 1), sc[i][2 * L:3 * L]) for i in idx]
        nr = [sc[i][L:2 * L] + pltpu.roll(sc[i][3 * L:], 2 * L, 1) for i in idx]
        rows_of = lambda *ms: jnp.concatenate([m.astype(BF16) for m in ms] + [zeros_b(LANES - L * len(ms))], axis=0)
        blocks = lambda m: rows_of(jnp.where(in_e, m, 0.0), jnp.where(in_o, m, 0.0))
        x = [x1[i][:L] + mm(nak[i].astype(BF16), rows_of(ev(v[i]), od(v[i]))) for i in idx]
        sq = lambda ms: [mm(m.astype(BF16), blocks(m)) for m in ms]
        p2 = sq(nab)
        p4 = sq(p2)
        eye2 = jnp.where((lane_l == row_l) | (lane_l == row_l + L), 1.0, 0.0)
        tinv = [eye2 + m for m in nab]
        tinv = [t + mm(t.astype(BF16), blocks(q)) for t, q in zip(tinv, p2)]
        tinv = [t + mm(t.astype(BF16), blocks(q)) for t, q in zip(tinv, p4)]
        if L > 8:
            p8 = sq(p4)
            tinv = [t + mm(t.astype(BF16), blocks(q)) for t, q in zip(tinv, p8)]
        sab = [mm(tinv[i].astype(BF16), rows_of(ev(x[i]), od(x[i]))).astype(BF16) for i in idx]
        for i, (n, p) in enumerate(tiles):
            y = x1[i][L:] + mm(nr[i].astype(BF16), rows_of(ev(sab[i]), ev(v[i]), od(sab[i]), od(v[i])))
            y_ref[n, rows, LANES * p:LANES * (p + 1)] = y
        for i, (n, p) in enumerate(tiles):
            upd = lax.dot_general(jnp.concatenate([b[i], k[i]], axis=0), jnp.concatenate([sab[i], v[i]], axis=0),
                                  (((0,), (0,)), ((), ())), preferred_element_type=F32)
            decay = jnp.broadcast_to(wl_all[n][:, LANES * p:LANES * (p + 1)], (LANES, LANES)).T
            s_ref[n, p] = decay * (s0[i] + jnp.where(same_head, upd, 0.0))
        return carry

    lax.fori_loop(0, tc // L, chunk, 0)


def _rwkv_chunk(at, rt, bt, kt, v, wl, tc):
    b, t, c = at.shape
    nb = 4 if b % 4 == 0 else (2 if b % 2 == 0 else 1)
    tok = pl.BlockSpec((nb, tc, c), lambda bi, i: (bi, i, 0))
    return pl.pallas_call(
        _rwkv_chunk_kernel,
        grid=(b // nb, t // tc),
        in_specs=[tok, tok, tok, tok, tok, pl.BlockSpec((nb, tc // REC_CHUNK, c), lambda bi, i: (bi, i, 0))],
        out_specs=tok,
        out_shape=jax.ShapeDtypeStruct((b, t, c), F32),
        scratch_shapes=[pltpu.VMEM((nb, RW_PAIRS, LANES, LANES), F32)],
        compiler_params=_cparams(("parallel", "arbitrary")),
        name="rwkv_chunk",
    )(at, rt, bt, kt, v, wl)


def _rwkv_post_kernel(y_ref, bonus_ref, g_ref, lnw_ref, lnb_ref, e_ref, et_ref, wb_ref, o_ref):
    y = y_ref[...]
    e, et = e_ref[...], et_ref[...]
    mean = _pick_dot(_pick_dot(y, e), et) * (1.0 / RW_HEAD)
    yc = y - mean
    var = _pick_dot(_pick_dot(yc * yc, e), et) * (1.0 / RW_HEAD)
    y = yc * lax.rsqrt(var + RW_GN_EPS) * lnw_ref[...] + lnb_ref[...]
    y = (y + bonus_ref[...]) * g_ref[...]
    o_ref[...] = _bdot(y, wb_ref[...])


def _rwkv_post(y, bonus, g, prm, tt):
    b, t, c = y.shape
    tok = pl.BlockSpec((None, tt, c), lambda bi, i: (bi, i, 0))
    names = ["ln_w", "ln_b", "seg", "seg_t", "w_branch"]
    return pl.pallas_call(
        _rwkv_post_kernel,
        grid=(b, t // tt),
        in_specs=[tok, tok, tok] + [_full(prm[nm].shape) for nm in names],
        out_specs=pl.BlockSpec((None, tt, D_MODEL), lambda bi, i: (bi, i, 0)),
        out_shape=jax.ShapeDtypeStruct((b, t, D_MODEL), F32),
        compiler_params=_cparams(("parallel", "parallel")),
        name="rwkv_post",
    )(y, bonus, g, *[prm[nm] for nm in names])


def _ssd_kernel(z_ref, xbc_ref, dt_ref, cw_ref, cb_ref, dtb_ref, alog_ref, dskip_ref, ng_ref, ex_ref, tri_ref, wb_ref,
                o_ref, buf, st_ref):
    L = SSD_CHUNK
    first = pl.program_id(1) == 0

    @pl.when(first)
    def _():
        buf[0:SUBLANES, :] = jnp.zeros((SUBLANES, SSD_CONV_CH), F32)
        st_ref[...] = jnp.zeros_like(st_ref)

    buf[SUBLANES:, :] = xbc_ref[...]
    window = buf[...]
    conv = cb_ref[...] + window[SUBLANES:, :] * cw_ref[SSD_CONV - 1:SSD_CONV, :]
    for j in range(SSD_CONV - 1):
        lag = SSD_CONV - 1 - j
        conv = conv + pltpu.roll(window, lag, 0)[SUBLANES:, :] * cw_ref[j:j + 1, :]
    buf[0:SUBLANES, :] = window[L:, :]
    xa = conv * _sigmoid(conv)
    xs = xa[:, :SSD_WIDTH]

    dt = _softplus(dt_ref[...] + dtb_ref[...])
    a = dt * (-jnp.exp(alog_ref[...]))
    row = lax.broadcasted_iota(jnp.int32, (L, L), 0)
    col = lax.broadcasted_iota(jnp.int32, (L, L), 1)
    causal = row >= col
    acs = _pick_dot_left(tri_ref[...], a)
    acs_last = acs[L - 1:L, :]
    stacked = jnp.concatenate(
        [dt, jnp.exp(acs), jnp.exp(acs_last - acs), jnp.broadcast_to(jnp.exp(acs_last), (SUBLANES, LANES))], axis=0)
    wide = _pick_dot(stacked, ex_ref[...])
    dt_x, ea_x, dte_x, el_x = wide[:L], wide[L:2 * L], wide[2 * L:3 * L], wide[3 * L:3 * L + 1]
    acs_t = acs.T
    xdt = xs * dt_x
    lane = lax.broadcasted_iota(jnp.int32, (L, LANES), 1)
    is_lo = lane < SSD_HEAD

    y_parts = []
    for g in range(SSD_GROUPS):
        bg = xa[:, SSD_WIDTH + SSD_STATE * g:SSD_WIDTH + SSD_STATE * (g + 1)]
        cg = xa[:, SSD_WIDTH + SSD_STATE * (SSD_GROUPS + g):SSD_WIDTH + SSD_STATE * (SSD_GROUPS + g + 1)]
        cb = lax.dot_general(cg.astype(BF16), bg.astype(BF16), (((1,), (1,)), ((), ())), preferred_element_type=F32)
        gl = slice(SSD_GW * g, SSD_GW * (g + 1))
        y_off = _bdot(cg, st_ref[g]) * ea_x[:, gl]
        diag = []
        for pp in range(SSD_HG // 2):
            acc = None
            xp = xdt[:, SSD_GW * g + LANES * pp:SSD_GW * g + LANES * (pp + 1)]
            for half in range(2):
                h = SSD_HG * g + 2 * pp + half
                seg = acs[:, h:h + 1] - acs_t[h:h + 1, :]
                lm = jnp.exp(jnp.where(causal, seg, -BIG))
                xh = jnp.where(is_lo if half == 0 else jnp.logical_not(is_lo), xp, 0.0)
                term = _bdot(lm * cb, xh)
                acc = term if acc is None else acc + term
            diag.append(acc)
        y_parts.append(jnp.concatenate(diag, axis=1) + y_off)
        st_ref[g] = st_ref[g] * el_x[:, gl] + _bdot(bg.T, xdt[:, gl] * dte_x[:, gl])
    y = jnp.concatenate(y_parts, axis=1)
    y = y + xs * dskip_ref[...]
    z = z_ref[...]
    y = y * (z * _sigmoid(z))
    normed = []
    for g in range(SSD_GROUPS):
        yg = y[:, SSD_GW * g:SSD_GW * (g + 1)]
        ms = jnp.mean(yg * yg, axis=-1, keepdims=True)
        normed.append(yg * lax.rsqrt(ms + NORM_EPS))
    y = jnp.concatenate(normed, axis=1) * ng_ref[...]
    o_ref[...] = _bdot(y, wb_ref[...])


def _ssd(proj3, prm):
    b, t, _ = proj3.shape
    L = SSD_CHUNK
    tok = lambda w, off: pl.BlockSpec((None, L, w), lambda bi, i: (bi, i, off // w))
    names = ["conv_w", "conv_b", "dt_bias", "a_log", "d_skip", "norm_g", "expand", "tri", "w_branch"]
    return pl.pallas_call(
        _ssd_kernel,
        grid=(b, t // L),
        in_specs=[tok(W_Z, OFF_Z), tok(W_XBC, OFF_XBC), tok(W_DT, OFF_DT)] + [_full(prm[nm].shape) for nm in names],
        out_specs=pl.BlockSpec((None, L, D_MODEL), lambda bi, i: (bi, i, 0)),
        out_shape=jax.ShapeDtypeStruct((b, t, D_MODEL), F32),
        scratch_shapes=[pltpu.VMEM((L + SUBLANES, SSD_CONV_CH), F32), pltpu.VMEM((SSD_GROUPS, SSD_STATE, SSD_GW), F32)],
        compiler_params=_cparams(("parallel", "arbitrary")),
        name="ssd",
    )(proj3, proj3, proj3, *[prm[nm] for nm in names])


def _mix_kernel(ya_ref, yb_ref, gate_ref, h_ref, wo_ref, gf_ref, h_o, ut_o):
    gt = gate_ref[...]
    mixed = _sigmoid(gt[:, :D_MODEL]) * ya_ref[...] + _sigmoid(gt[:, D_MODEL:]) * yb_ref[...]
    h = h_ref[...] + _bdot(mixed, wo_ref[...])
    h_o[...] = h
    ms = jnp.mean(h * h, axis=-1, keepdims=True)
    u = h * lax.rsqrt(ms + NORM_EPS) * gf_ref[...]
    ut_o[...] = pltpu.bitcast(u.T.astype(BF16), jnp.uint32)


def _mix(ya, yb, proj, h, prm, tm):
    n = h.shape[0]
    names = ["w_out", "norm_ffn_g"]
    tok = lambda w, c: pl.BlockSpec((tm, w), lambda i: (i, c))
    return pl.pallas_call(
        _mix_kernel,
        grid=(n // tm,),
        in_specs=[tok(D_MODEL, 0), tok(D_MODEL, 0), tok(W_GATE, OFF_GATE // W_GATE), tok(D_MODEL, 0)]
        + [_full(prm[nm].shape) for nm in names],
        out_specs=[tok(D_MODEL, 0), pl.BlockSpec((D_MODEL // 2, tm), lambda i: (0, i))],
        out_shape=[jax.ShapeDtypeStruct((n, D_MODEL), F32), jax.ShapeDtypeStruct((D_MODEL // 2, n), jnp.uint32)],
        compiler_params=_cparams(("parallel",)),
        name="mix",
    )(ya, yb, proj, h, *[prm[nm] for nm in names])


def _top_rows(x, n, want_rank):
    tm = x.shape[1]
    row_n = lax.broadcasted_iota(jnp.int32, (n, tm), 0)
    tops = jnp.zeros((n, tm), F32)
    rank = jnp.full(x.shape, float(n), F32) if want_rank else None
    m = None
    for k in range(n):
        m = jnp.max(x, axis=0, keepdims=True)
        tops = jnp.where(row_n == k, m, tops)
        hit = x >= m
        if want_rank:
            rank = jnp.where(hit, float(k), rank)
        if k + 1 < n:
            x = jnp.where(hit, -BIG, x)
    return tops, m, rank


def _peer_sel_kernel(ut_ref, wq_ref, keys_ref, e1_o, cnt_o, e2_o, r2_o):
    k = PEER_TOPK
    qt = jnp.dot(wq_ref[...], pltpu.bitcast(ut_ref[...], BF16), preferred_element_type=F32)
    for h in range(PEER_HEADS):
        s = []
        for c in range(2):
            j = 2 * h + c
            s.append(_bdot(keys_ref[j], qt[LANES * j:LANES * (j + 1), :]))
        s1, s2 = s
        top1, thr1, _ = _top_rows(s1, k, False)
        top2, _, rank2 = _top_rows(s2, k, True)
        in1 = s1 >= thr1
        in2 = rank2 < float(k)
        row8 = lax.broadcasted_iota(jnp.int32, (SUBLANES, s1.shape[1]), 0)
        groups = [top1[0:1, :] + top2]
        for k1 in range(1, SUBLANES):
            width = k // (k1 + 1)
            groups.append(jnp.where(row8 < width, top1[k1:k1 + 1, :] + top2[:SUBLANES, :], -BIG))
        groups.append(top1[SUBLANES:, :] + top2[0:1, :])
        x = jnp.concatenate(groups, axis=0)
        m0 = zsum = m = None
        for i in range(k):
            m = jnp.max(x, axis=0, keepdims=True)
            if i == 0:
                m0, zsum = m, jnp.ones_like(m)
            else:
                zsum = zsum + jnp.exp(m - m0)
            if i + 1 < k:
                x = jnp.where(x >= m, -BIG, x)
        thr = m
        cnt_rank = jnp.zeros_like(top1)
        for k2 in range(k):
            cnt_rank = cnt_rank + jnp.where(top1 + top2[k2:k2 + 1, :] >= thr, 1.0, 0.0)
        cnt = jnp.zeros_like(s1)
        for k1 in range(k):
            cnt = jnp.where(s1 == top1[k1:k1 + 1, :], cnt_rank[k1:k1 + 1, :], cnt)
        e1_o[h] = jnp.where(in1, jnp.exp(s1 - top1[0:1, :]), 0.0) / zsum
        cnt_o[h] = cnt
        e2_o[h] = pltpu.bitcast(jnp.where(in2, jnp.exp(s2 - top2[0:1, :]), 0.0).astype(BF16), jnp.uint32)
        r2_o[h] = pltpu.bitcast(rank2.astype(BF16), jnp.uint32)


def _peer_sel(ut, prm, tm):
    n = ut.shape[1]
    shp = (PEER_HEADS, PEER_NKEYS, n)
    ospec = pl.BlockSpec((PEER_HEADS, PEER_NKEYS, tm), lambda i: (0, 0, i))
    pshp = (PEER_HEADS, PEER_NKEYS // 2, n)
    pspec = pl.BlockSpec((PEER_HEADS, PEER_NKEYS // 2, tm), lambda i: (0, 0, i))
    return pl.pallas_call(
        _peer_sel_kernel,
        grid=(n // tm,),
        in_specs=[pl.BlockSpec((D_MODEL // 2, tm), lambda i: (0, i)), _full(prm["wq_t"].shape), _full(prm["keys"].shape)],
        out_specs=[ospec, ospec, pspec, pspec],
        out_shape=[jax.ShapeDtypeStruct(shp, F32), jax.ShapeDtypeStruct(shp, F32),
                   jax.ShapeDtypeStruct(pshp, jnp.uint32), jax.ShapeDtypeStruct(pshp, jnp.uint32)],
        compiler_params=_cparams(("parallel",)),
        name="peer_sel",
    )(ut, prm["wq_t"], prm["keys"])


def _gated_gelu(x, gate):
    c = 0.7978845608028654
    t = jnp.tanh(x * (c + (c * 0.044715) * (x * x)))
    hx = x.astype(BF16) * 0.5
    return (hx + hx * t.astype(BF16)) * gate


def _peer_ffn_kernel(final_g, ut_ref, down_ref, upt_ref, uptp_ref, e1_ref, cnt_ref, e2_ref, r2_ref, h_ref, *rest):
    if final_g:
        gfin_ref, o_ref, *scr = rest
    else:
        o_ref, *scr = rest
    nh = len(scr) // 5
    acc, act, gbuf, hs, tok = (scr[nh * q:nh * (q + 1)] for q in range(5))
    j = pl.program_id(1)
    te, th = act[0].shape
    sub = 2 * SUBLANES

    @pl.when(j == 0)
    def _():
        for hf in range(nh):
            acc[hf][...] = jnp.zeros_like(acc[hf])
            tok[hf][...] = pltpu.bitcast(ut_ref[:, th * hf:th * (hf + 1)], BF16)
        hs[nh - 1][...] = jnp.zeros_like(hs[nh - 1])

    kb = 2
    blk = kb * PEER_NKEYS
    nblk = te // blk
    nm = D_MODEL // blk
    sbt = 4

    def project_down(hf, b):
        rows = slice(blk * b, blk * (b + 1))
        lhs = pltpu.bitcast(down_ref[blk // 2 * b:blk // 2 * (b + 1), :], BF16)
        act[hf][rows, :] = jnp.dot(lhs, tok[hf][...], preferred_element_type=F32)

    def build_gates(hf, b):
        for c in range(th // LANES):
            cl = slice(th * hf + LANES * c, th * hf + LANES * (c + 1))
            first = lambda ref, h, q: jnp.broadcast_to(ref[h, kb * b + q:kb * b + q + 1, cl], (sub, LANES)).astype(BF16)
            cnt = [[first(cnt_ref, h, q) for q in range(kb)] for h in range(PEER_HEADS)]
            e1 = [[first(e1_ref, h, q) for q in range(kb)] for h in range(PEER_HEADS)]
            for s0 in range(0, PEER_NKEYS // sub, sbt):
                gate = [[jnp.zeros((sub, LANES), BF16) for _ in range(sbt)] for _ in range(kb)]
                for h in range(PEER_HEADS):
                    r2 = [pltpu.bitcast(r2_ref[h, SUBLANES * (s0 + s):SUBLANES * (s0 + s + 1), cl], BF16) for s in range(sbt)]
                    e2 = [pltpu.bitcast(e2_ref[h, SUBLANES * (s0 + s):SUBLANES * (s0 + s + 1), cl], BF16) for s in range(sbt)]
                    for q in range(kb):
                        for s in range(sbt):
                            gate[q][s] = gate[q][s] + jnp.where(r2[s] < cnt[h][q], e2[s], 0.0) * e1[h][q]
                for q in range(kb):
                    for s in range(sbt):
                        gw = (blk * b + PEER_NKEYS * q + sub * (s0 + s)) // 2
                        gbuf[hf][gw:gw + SUBLANES, LANES * c:LANES * (c + 1)] = pltpu.bitcast(gate[q][s], jnp.uint32)

    def activate(hf, b):
        for r0 in range(blk * b, blk * (b + 1), sub):
            for c in range(th // LANES):
                cl = slice(LANES * c, LANES * (c + 1))
                gate = pltpu.bitcast(gbuf[hf][r0 // 2:r0 // 2 + SUBLANES, cl], BF16)
                hs[hf][r0:r0 + sub, cl] = _gated_gelu(act[hf][r0:r0 + sub, cl], gate)

    def project_up(w_ref, hf, m):
        rows = slice(blk * m, blk * (m + 1))
        lhs = pltpu.bitcast(w_ref[blk // 2 * m:blk // 2 * (m + 1), :], BF16)
        acc[hf][rows, :] += jnp.dot(lhs, hs[hf][...], preferred_element_type=F32)

    for hf in range(nh):
        for b in range(nblk):
            project_down(hf, b)
            for m in range(b * nm // nblk, (b + 1) * nm // nblk):
                if hf == 0:
                    project_up(uptp_ref, nh - 1, m)
                else:
                    project_up(upt_ref, hf - 1, m)
            build_gates(hf, b)
            if b > 0:
                activate(hf, b - 1)
        activate(hf, nblk - 1)

    @pl.when(j == pl.num_programs(1) - 1)
    def _():
        for m in range(nm):
            project_up(upt_ref, nh - 1, m)
        for hf in range(nh):
            h = h_ref[th * hf:th * (hf + 1), :] + acc[hf][...].T
            if final_g:
                ms = jnp.mean(h * h, axis=-1, keepdims=True)
                h = h * lax.rsqrt(ms + NORM_EPS) * gfin_ref[...]
            o_ref[th * hf:th * (hf + 1), :] = h


def _peer_ffn(ut, down, upt, e1, cnt, e2, r2, h, final_g, tl, te):
    n = h.shape[0]
    nh = 2 if tl % (2 * LANES) == 0 else 1
    sel = pl.BlockSpec((PEER_HEADS, PEER_NKEYS // 2, tl), lambda i, j: (0, 0, i))
    sel1 = pl.BlockSpec((PEER_HEADS, te // PEER_NKEYS, tl), lambda i, j: (0, j, i))
    in_specs = [
        pl.BlockSpec((D_MODEL // 2, tl), lambda i, j: (0, i)),
        pl.BlockSpec((te // 2, D_MODEL), lambda i, j: (j, 0)),
        pl.BlockSpec((D_MODEL // 2, te), lambda i, j: (0, j)),
        pl.BlockSpec((D_MODEL // 2, te), lambda i, j: (0, jnp.maximum(j - 1, 0))),
        sel1, sel1, sel, sel,
        pl.BlockSpec((tl, D_MODEL), lambda i, j: (i, 0)),
    ]
    args = [ut, down, upt, upt, e1, cnt, e2, r2, h]
    if final_g is not None:
        in_specs.append(_full(final_g.shape))
        args.append(final_g)
    return pl.pallas_call(
        functools.partial(_peer_ffn_kernel, final_g is not None),
        grid=(n // tl, PEER_EXPERTS // te),
        in_specs=in_specs,
        out_specs=pl.BlockSpec((tl, D_MODEL), lambda i, j: (i, 0)),
        out_shape=jax.ShapeDtypeStruct((n, D_MODEL), F32),
        scratch_shapes=([pltpu.VMEM((D_MODEL, tl // nh), F32)] * nh + [pltpu.VMEM((te, tl // nh), F32)] * nh
                        + [pltpu.VMEM((te // 2, tl // nh), jnp.uint32)] * nh + [pltpu.VMEM((te, tl // nh), BF16)] * nh
                        + [pltpu.VMEM((D_MODEL, tl // nh), BF16)] * nh),
        compiler_params=_cparams(("parallel", "arbitrary")),
        name="peer_ffn",
    )(*args)


def _pack_weight_kernel(transpose, w_ref, o_ref):
    w = w_ref[...]
    if transpose:
        w = w.T
    o_ref[...] = pltpu.bitcast(w.astype(BF16), jnp.uint32)


def _pack_weight(w_layers, layer, transpose):
    _, r, c = w_layers.shape
    t = D_MODEL
    out_shape, out_map = ((c // 2, r), lambda i, j: (j, i)) if transpose else ((r // 2, c), lambda i, j: (i, j))
    return pl.pallas_call(
        functools.partial(_pack_weight_kernel, transpose),
        grid=(r // t, c // t),
        in_specs=[pl.BlockSpec((None, t, t), lambda i, j: (layer, i, j))],
        out_specs=pl.BlockSpec((t // 2, t), out_map),
        out_shape=jax.ShapeDtypeStruct(out_shape, jnp.uint32),
        compiler_params=_cparams(("parallel", "parallel")),
        name="pack_weight",
    )(w_layers)


def _pad_rows(w, start, total):
    return jnp.zeros((total, w.shape[1]), w.dtype).at[start:start + w.shape[0]].set(w)


def _pad_lanes(v, total):
    return jnp.zeros((1, total), v.dtype).at[0, :v.shape[0]].set(v)


def _row(v):
    return v.reshape(1, -1)


def _constants():
    c = jnp.arange(RW_WIDTH)[:, None] // RW_HEAD == jnp.arange(LANES)[None, :]
    seg = c.astype(BF16)
    expand = (jnp.arange(LANES)[:, None] == jnp.arange(SSD_WIDTH)[None, :] // SSD_HEAD).astype(BF16)
    return seg, seg.T, expand


def kernel(x, norm_mix_g, w_in, rwkv_mu, rwkv_w0, rwkv_w2, rwkv_a0, rwkv_a2, rwkv_g2, rwkv_v0, rwkv_v1, rwkv_v2,
           rwkv_k_k, rwkv_k_a, rwkv_r_k, rwkv_ln_w, rwkv_ln_b, w_rwkv_branch, ssd_conv_w, ssd_conv_b, ssd_dt_bias,
           ssd_A_log, ssd_D, ssd_norm_g, w_ssd_branch, w_out, norm_ffn_g, peer_w_q, peer_sub_keys, peer_down, peer_up,
           norm_final_g):
    bsz, t, d = x.shape
    n = bsz * t
    depth = w_in.shape[0]
    seg, seg_t, expand = _constants()
    h = x.reshape(n, d)
    v_first = None
    o_lora = 3 * RW_WIDTH
    o_z = o_lora + RW_LORA
    o_xbc = o_z + SSD_WIDTH
    o_dt = o_xbc + SSD_CONV_CH
    o_gate = o_dt + SSD_HEADS
    tt = min(TOK_TILE, t)
    step = jnp.arange(tt)
    chunk_tri = ((step[:, None] // REC_CHUNK == step[None, :] // REC_CHUNK) & (step[:, None] >= step[None, :])).astype(BF16)
    chunk_last = (jnp.arange(tt // REC_CHUNK)[:, None] * REC_CHUNK + REC_CHUNK - 1 == step[None, :]).astype(BF16)
    ssd_tri = (jnp.arange(SSD_CHUNK)[:, None] >= jnp.arange(SSD_CHUNK)[None, :]).astype(BF16)
    for l in range(depth):
        wl = w_in[l]
        zpad = lambda k: jnp.zeros((d, k), wl.dtype)
        w_cat = jnp.concatenate([
            wl[:, :o_lora], wl[:, o_xbc:o_dt], wl[:, o_z:o_xbc], wl[:, o_gate:],
            wl[:, o_dt:o_gate], zpad(LANES - SSD_HEADS), wl[:, o_lora:o_z], zpad(RW_LORA_PAD - RW_LORA)], axis=1).astype(BF16)
        proj = _norm_matmul(h, _row(norm_mix_g[l]), w_cat, tm=min(PROJ_TILE[0], n), tn=PROJ_TILE[1])
        proj3 = proj.reshape(bsz, t, PROJ_PAD)

        mu = rwkv_mu[l]
        rw = {
            "mu1": _row(mu[:o_lora]), "mu2": _pad_lanes(mu[o_lora:], RW_LORA_PAD),
            "w0": _row(rwkv_w0[l]), "w2": _pad_rows(rwkv_w2[l], 0, RW_LORA_PAD).astype(BF16),
            "a0": _row(rwkv_a0[l]), "a2": _pad_rows(rwkv_a2[l], DECAY_LORA, RW_LORA_PAD).astype(BF16),
            "g2": _pad_rows(rwkv_g2[l], DECAY_LORA + ICLR_LORA, RW_LORA_PAD).astype(BF16),
            "k_k": _row(rwkv_k_k[l]), "k_a": _row(rwkv_k_a[l]), "r_k": _row(rwkv_r_k[l]),
            "seg": seg, "seg_t": seg_t, "chunk_tri": chunk_tri, "chunk_last": chunk_last,
            "ln_w": _row(rwkv_ln_w[l]), "ln_b": _row(rwkv_ln_b[l]), "w_branch": w_rwkv_branch[l].astype(BF16),
        }
        if l > 0:
            rw["v0"] = _row(rwkv_v0[l - 1])
            rw["v1"] = jnp.zeros((RW_WIDTH, VRES_PAD), F32).at[:, :VRES_LORA].set(rwkv_v1[l - 1]).astype(BF16)
            rw["v2"] = _pad_rows(rwkv_v2[l - 1], 0, VRES_PAD).astype(BF16)
        outs = _rwkv_prep(proj3, v_first, rw, tt)
        at, rt, bt, kt, vb, w_end, g_, bonus = outs[:8]
        if l == 0:
            v_first = outs[8]
        y_r = _rwkv_chunk(at, rt, bt, kt, vb, w_end, tc=tt)
        y_a = _rwkv_post(y_r, bonus, g_, rw, tt)

        sp = {
            "conv_w": ssd_conv_w[l], "conv_b": _row(ssd_conv_b[l]),
            "dt_bias": _pad_lanes(ssd_dt_bias[l], LANES), "a_log": _pad_lanes(ssd_A_log[l], LANES),
            "d_skip": _row(jnp.repeat(ssd_D[l], SSD_HEAD)), "norm_g": _row(ssd_norm_g[l]),
            "expand": expand, "tri": ssd_tri, "w_branch": w_ssd_branch[l].astype(BF16),
        }
        y_b = _ssd(proj3, sp)

        pp = {
            "w_out": w_out[l].astype(BF16), "norm_ffn_g": _row(norm_ffn_g[l]), "wq_t": peer_w_q[l].T.astype(BF16),
            "keys": peer_sub_keys[l].reshape(2 * PEER_HEADS, PEER_NKEYS, PEER_QDIM // 2).astype(BF16),
        }
        h, ut = _mix(y_a.reshape(n, d), y_b.reshape(n, d), proj, h, pp, tm=min(TOK_TILE, n))
        e1, cnt, e2, r2 = _peer_sel(ut, pp, tm=min(TOK_TILE, n))
        fin = _row(norm_final_g) if l == depth - 1 else None
        h = _peer_ffn(ut, _pack_weight(peer_down, l, False), _pack_weight(peer_up, l, True), e1, cnt, e2, r2, h, fin,
                      tl=min(FFN_TILE[0], n), te=FFN_TILE[1])
    return h.reshape(bsz, t, d)
```

```python
import functools

import jax
import jax.numpy as jnp
from jax import lax
from jax.experimental import pallas as pl
from jax.experimental.pallas import tpu as pltpu

F32 = jnp.float32
BF16 = jnp.bfloat16

LANES = 128
SUBLANES = 8
NORM_EPS = 1e-6

D_MODEL = 1024
RW_HEAD = 64
RW_WIDTH = D_MODEL
RW_HEADS = RW_WIDTH // RW_HEAD
RW_PAIRS = RW_WIDTH // LANES
DECAY_LORA, ICLR_LORA, GATE_LORA, VRES_LORA = 64, 64, 160, 32
RW_LORA = DECAY_LORA + ICLR_LORA + GATE_LORA
RW_LORA_PAD = 384
VRES_PAD = 128
RW_GN_EPS = 64e-5
REC_CHUNK = 16
SSD_WIDTH = 2 * D_MODEL
SSD_HEAD = 64
SSD_HEADS = SSD_WIDTH // SSD_HEAD
SSD_STATE = 128
SSD_GROUPS = 4
SSD_GW = SSD_WIDTH // SSD_GROUPS
SSD_HG = SSD_HEADS // SSD_GROUPS
SSD_CONV = 4
SSD_CONV_CH = SSD_WIDTH + 2 * SSD_GROUPS * SSD_STATE
SSD_CHUNK = 128
PEER_HEADS = 8
PEER_NKEYS = 128
PEER_TOPK = 16
PEER_QDIM = 256
PEER_EXPERTS = PEER_NKEYS * PEER_NKEYS
BIG = 1e30

OFF_RKV, W_RKV = 0, 3 * RW_WIDTH
OFF_XBC, W_XBC = 3072, SSD_CONV_CH
OFF_Z, W_Z = 6144, SSD_WIDTH
OFF_GATE, W_GATE = 8192, 2 * D_MODEL
OFF_DT, W_DT = 10240, LANES
OFF_LORA, W_LORA = 10368, RW_LORA_PAD
PROJ_PAD = 10752

VMEM_LIMIT = 56 * 1024 * 1024
TOK_TILE = 256
PROJ_TILE = (1024, 1792)
FFN_TILE = (512, 16 * PEER_NKEYS)


def _cparams(sem):
    return pltpu.CompilerParams(dimension_semantics=sem, vmem_limit_bytes=VMEM_LIMIT)


def _full(shape):
    n = len(shape)
    return pl.BlockSpec(shape, lambda *_: (0,) * n)


def _sigmoid(x):
    return 1.0 / (1.0 + jnp.exp(-x))


def _softplus(x):
    return jnp.maximum(x, 0.0) + jnp.log(1.0 + jnp.exp(-jnp.abs(x)))


def _bdot(a, b):
    return jnp.dot(a.astype(BF16), b.astype(BF16), preferred_element_type=F32)


def _split(x):
    hi = x.astype(BF16)
    return hi, (x - hi.astype(F32)).astype(BF16)


def _pick_dot(x, sel):
    hi, lo = _split(x)
    return jnp.dot(hi, sel, preferred_element_type=F32) + jnp.dot(lo, sel, preferred_element_type=F32)


def _pick_dot_left(sel, x):
    hi, lo = _split(x)
    return jnp.dot(sel, hi, preferred_element_type=F32) + jnp.dot(sel, lo, preferred_element_type=F32)


def _norm_matmul_kernel(x_ref, g_ref, w_ref, o_ref, u_ref):
    @pl.when(pl.program_id(1) == 0)
    def _():
        x = x_ref[...]
        ms = jnp.mean(x * x, axis=-1, keepdims=True)
        u_ref[...] = (x * lax.rsqrt(ms + NORM_EPS) * g_ref[...]).astype(BF16)

    o_ref[...] = jnp.dot(u_ref[...], w_ref[...], preferred_element_type=F32)


def _norm_matmul(x, g, w, tm, tn):
    n, d = x.shape
    m = w.shape[1]
    return pl.pallas_call(
        _norm_matmul_kernel,
        grid=(n // tm, m // tn),
        in_specs=[
            pl.BlockSpec((tm, d), lambda i, j: (i, 0)),
            pl.BlockSpec((1, d), lambda i, j: (0, 0)),
            pl.BlockSpec((d, tn), lambda i, j: (0, j)),
        ],
        out_specs=pl.BlockSpec((tm, tn), lambda i, j: (i, j)),
        out_shape=jax.ShapeDtypeStruct((n, m), F32),
        scratch_shapes=[pltpu.VMEM((tm, d), BF16)],
        compiler_params=_cparams(("parallel", "arbitrary")),
        name="in_proj",
    )(x, g, w)


def _shift_mix(x, prev_ref, mu):
    rows = lax.broadcasted_iota(jnp.int32, x.shape, 0)
    xs = jnp.where(rows == 0, prev_ref[...], pltpu.roll(x, 1, 0))
    prev_ref[...] = x[x.shape[0] - 1:, :]
    return x + (xs - x) * mu


def _rwkv_prep_kernel(has_vres, *refs):
    if has_vres:
        (prkv_ref, plora_ref, vf_ref, mu1_ref, mu2_ref, w0_ref, w2_ref, a0_ref, a2_ref, g2_ref,
         kk_ref, ka_ref, rk_ref, e_ref, et_ref, ctri_ref, csel_ref, v0_ref, v1_ref, v2_ref,
         at_o, rt_o, bt_o, kt_o, vb_o, wl_o, g_o, bonus_o, prev1, prev2) = refs
    else:
        (prkv_ref, plora_ref, mu1_ref, mu2_ref, w0_ref, w2_ref, a0_ref, a2_ref, g2_ref,
         kk_ref, ka_ref, rk_ref, e_ref, et_ref, ctri_ref, csel_ref,
         at_o, rt_o, bt_o, kt_o, vb_o, wl_o, g_o, bonus_o, v_o, prev1, prev2) = refs
    @pl.when(pl.program_id(1) == 0)
    def _():
        prev1[...] = jnp.zeros_like(prev1)
        prev2[...] = jnp.zeros_like(prev2)

    p = _shift_mix(prkv_ref[...], prev1, mu1_ref[...])
    lo = _shift_mix(plora_ref[...], prev2, mu2_ref[...])
    r = p[:, :RW_WIDTH]
    k = p[:, RW_WIDTH:2 * RW_WIDTH]
    v = p[:, 2 * RW_WIDTH:]
    w_log = -_softplus(-(w0_ref[...] + _bdot(jnp.tanh(lo), w2_ref[...]))) - 0.5
    log_w = -jnp.exp(w_log)
    a = _sigmoid(a0_ref[...] + _bdot(lo, a2_ref[...]))
    g = _bdot(_sigmoid(lo), g2_ref[...])
    if has_vres:
        mix = _sigmoid(v0_ref[...] + _bdot(_bdot(v, v1_ref[...]), v2_ref[...]))
        v = v + (vf_ref[...] - v) * mix
    else:
        v_o[...] = v
    e, et = e_ref[...], et_ref[...]
    kk = k * kk_ref[...]
    inv = lax.rsqrt(_pick_dot(kk * kk, e) + 1e-12)
    kk = kk * _pick_dot(inv, et)
    k2 = k * (1.0 + (a - 1.0) * ka_ref[...])
    rk = _pick_dot(_pick_dot(r * k2 * rk_ref[...], e), et)
    cum = _pick_dot_left(ctri_ref[...], log_w)
    grow, shrink = jnp.exp(cum), jnp.exp(-cum)
    at_o[...] = (-kk * jnp.exp(cum - log_w)).astype(BF16)
    rt_o[...] = (r * grow).astype(BF16)
    bt_o[...] = (kk * a * shrink).astype(BF16)
    kt_o[...] = (k2 * shrink).astype(BF16)
    vb_o[...] = v.astype(BF16)
    wl_o[...] = jnp.exp(_pick_dot_left(csel_ref[...], cum))
    g_o[...] = g
    bonus_o[...] = rk * v


def _rwkv_prep(proj3, vfirst, prm, tt):
    b, t, _ = proj3.shape
    has_vres = vfirst is not None
    tok = lambda w, c: pl.BlockSpec((None, tt, w), lambda bi, i: (bi, i, c))
    in_specs = [tok(W_RKV, OFF_RKV // W_RKV), tok(W_LORA, OFF_LORA // W_LORA)]
    args = [proj3, proj3]
    if has_vres:
        in_specs.append(tok(RW_WIDTH, 0))
        args.append(vfirst)
    names = ["mu1", "mu2", "w0", "w2", "a0", "a2", "g2", "k_k", "k_a", "r_k", "seg", "seg_t", "chunk_tri", "chunk_last"]
    if has_vres:
        names += ["v0", "v1", "v2"]
    for nm in names:
        in_specs.append(_full(prm[nm].shape))
        args.append(prm[nm])
    out_shape = [jax.ShapeDtypeStruct((b, t, RW_WIDTH), BF16)] * 5
    out_specs = [tok(RW_WIDTH, 0)] * 5
    out_shape.append(jax.ShapeDtypeStruct((b, t // REC_CHUNK, RW_WIDTH), F32))
    out_specs.append(pl.BlockSpec((None, tt // REC_CHUNK, RW_WIDTH), lambda bi, i: (bi, i, 0)))
    out_shape += [jax.ShapeDtypeStruct((b, t, RW_WIDTH), F32)] * 2
    out_specs += [tok(RW_WIDTH, 0)] * 2
    if not has_vres:
        out_shape.append(jax.ShapeDtypeStruct((b, t, RW_WIDTH), F32))
        out_specs.append(tok(RW_WIDTH, 0))
    return pl.pallas_call(
        functools.partial(_rwkv_prep_kernel, has_vres),
        grid=(b, t // tt),
        in_specs=in_specs,
        out_specs=out_specs,
        out_shape=out_shape,
        scratch_shapes=[pltpu.VMEM((1, W_RKV), F32), pltpu.VMEM((1, W_LORA), F32)],
        compiler_params=_cparams(("parallel", "arbitrary")),
        name="rwkv_prep",
    )(*args)


def _rwkv_chunk_kernel(at_ref, rt_ref, bt_ref, kt_ref, v_ref, wl_ref, y_ref, s_ref):
    nb, tc = at_ref.shape[0], at_ref.shape[1]
    L = REC_CHUNK
    pairs = range(RW_PAIRS)

    @pl.when(pl.program_id(1) == 0)
    def _():
        s_ref[...] = jnp.zeros_like(s_ref)

    lane_l = lax.broadcasted_iota(jnp.int32, (L, LANES), 1)
    row_l = lax.broadcasted_iota(jnp.int32, (L, LANES), 0)
    is_e = lane_l < RW_HEAD
    row_p = lax.broadcasted_iota(jnp.int32, (4 * L, LANES), 0)
    lane_p = lax.broadcasted_iota(jnp.int32, (4 * L, LANES), 1)
    t_p, j_p = row_p % L, lane_p % L
    keep = (lane_p < 2 * L) & ((j_p < t_p) | (((row_p // L) % 2 == 1) & (j_p == t_p)))
    row_s = lax.broadcasted_iota(jnp.int32, (LANES, LANES), 0)
    lane_s = lax.broadcasted_iota(jnp.int32, (LANES, LANES), 1)
    same_head = (row_s < RW_HEAD) == (lane_s < RW_HEAD)
    zeros_b = lambda r: jnp.zeros((r, LANES), BF16)
    mm = lambda a, b: jnp.dot(a, b, preferred_element_type=F32)

    def chunk(j, carry):
        rows = pl.ds(pl.multiple_of(j * L, L), L)
        grp = pl.ds(pl.multiple_of((j // SUBLANES) * SUBLANES, SUBLANES), SUBLANES)
        pick = lax.broadcasted_iota(jnp.int32, (SUBLANES, RW_WIDTH), 0) == j % SUBLANES
        tiles = [(n, p) for n in range(nb) for p in pairs]
        wl_all = [jnp.sum(jnp.where(pick, wl_ref[n, grp, :], 0.0), axis=0, keepdims=True) for n in range(nb)]
        ld = lambda ref: [ref[n, rows, LANES * p:LANES * (p + 1)] for n, p in tiles]
        a, r, b, k, v = ld(at_ref), ld(rt_ref), ld(bt_ref), ld(kt_ref), ld(v_ref)
        ev = lambda m: jnp.where(is_e, m, jnp.zeros_like(m))
        od = lambda m: jnp.where(is_e, jnp.zeros_like(m), m)
        idx = range(len(tiles))
        sc = []
        for i in idx:
            lhs = jnp.concatenate([ev(a[i]), ev(r[i]), od(a[i]), od(r[i])], axis=0)
            rhs = jnp.concatenate([b[i], k[i], zeros_b(LANES - 2 * L)], axis=0)
            s_i = lax.dot_general(lhs, rhs, (((1,), (1,)), ((), ())), preferred_element_type=F32)
            sc.append(jnp.where(keep, s_i, 0.0))
        s0 = [s_ref[n, p] for n, p in tiles]
        x1 = [mm(jnp.concatenate([a[i], r[i]], axis=0), s0[i].astype(BF16)) for i in idx]
        in_e, in_o = lane_l < L, (lane_l >= L) & (lane_l < 2 * L)
        side = lambda m_e, m_o: jnp.where(in_e, m_e, 0.0) + jnp.where(in_o, m_o, 0.0)
        nab = [side(sc[i][:L], pltpu.roll(sc[i][2 * L:3 * L], L, 1)) for i in idx]
        nak = [side(pltpu.roll(sc[i][:L], LANES - L, 1), sc[i][2 * L:3 * L]) for i in idx]
        nr = [sc[i][L:2 * L] + pltpu.roll(sc[i][3 * L:], 2 * L, 1) for i in idx]
        rows_of = lambda *ms: jnp.concatenate([m.astype(BF16) for m in ms] + [zeros_b(LANES - L * len(ms))], axis=0)
        blocks = lambda m: rows_of(jnp.where(in_e, m, 0.0), jnp.where(in_o, m, 0.0))
        x = [x1[i][:L] + mm(nak[i].astype(BF16), rows_of(ev(v[i]), od(v[i]))) for i in idx]
        sq = lambda ms: [mm(m.astype(BF16), blocks(m)) for m in ms]
        p2 = sq(nab)
        p4 = sq(p2)
        eye2 = jnp.where((lane_l == row_l) | (lane_l == row_l + L), 1.0, 0.0)
        tinv = [eye2 + m for m in nab]
        tinv = [t + mm(t.astype(BF16), blocks(q)) for t, q in zip(tinv, p2)]
        tinv = [t + mm(t.astype(BF16), blocks(q)) for t, q in zip(tinv, p4)]
        if L > 8:
            p8 = sq(p4)
            tinv = [t + mm(t.astype(BF16), blocks(q)) for t, q in zip(tinv, p8)]
        sab = [mm(tinv[i].astype(BF16), rows_of(ev(x[i]), od(x[i]))).astype(BF16) for i in idx]
        for i, (n, p) in enumerate(tiles):
            y = x1[i][L:] + mm(nr[i].astype(BF16), rows_of(ev(sab[i]), ev(v[i]), od(sab[i]), od(v[i])))
            y_ref[n, rows, LANES * p:LANES * (p + 1)] = y
        for i, (n, p) in enumerate(tiles):
            upd = lax.dot_general(jnp.concatenate([b[i], k[i]], axis=0), jnp.concatenate([sab[i], v[i]], axis=0),
                                  (((0,), (0,)), ((), ())), preferred_element_type=F32)
            decay = jnp.broadcast_to(wl_all[n][:, LANES * p:LANES * (p + 1)], (LANES, LANES)).T
            s_ref[n, p] = decay * (s0[i] + jnp.where(same_head, upd, 0.0))
        return carry

    lax.fori_loop(0, tc // L, chunk, 0)


def _rwkv_chunk(at, rt, bt, kt, v, wl, tc):
    b, t, c = at.shape
    nb = 4 if b % 4 == 0 else (2 if b % 2 == 0 else 1)
    tok = pl.BlockSpec((nb, tc, c), lambda bi, i: (bi, i, 0))
    return pl.pallas_call(
        _rwkv_chunk_kernel,
        grid=(b // nb, t // tc),
        in_specs=[tok, tok, tok, tok, tok, pl.BlockSpec((nb, tc // REC_CHUNK, c), lambda bi, i: (bi, i, 0))],
        out_specs=tok,
        out_shape=jax.ShapeDtypeStruct((b, t, c), F32),
        scratch_shapes=[pltpu.VMEM((nb, RW_PAIRS, LANES, LANES), F32)],
        compiler_params=_cparams(("parallel", "arbitrary")),
        name="rwkv_chunk",
    )(at, rt, bt, kt, v, wl)


def _rwkv_post_kernel(y_ref, bonus_ref, g_ref, lnw_ref, lnb_ref, e_ref, et_ref, wb_ref, o_ref):
    y = y_ref[...]
    e, et = e_ref[...], et_ref[...]
    mean = _pick_dot(_pick_dot(y, e), et) * (1.0 / RW_HEAD)
    yc = y - mean
    var = _pick_dot(_pick_dot(yc * yc, e), et) * (1.0 / RW_HEAD)
    y = yc * lax.rsqrt(var + RW_GN_EPS) * lnw_ref[...] + lnb_ref[...]
    y = (y + bonus_ref[...]) * g_ref[...]
    o_ref[...] = _bdot(y, wb_ref[...])


def _rwkv_post(y, bonus, g, prm, tt):
    b, t, c = y.shape
    tok = pl.BlockSpec((None, tt, c), lambda bi, i: (bi, i, 0))
    names = ["ln_w", "ln_b", "seg", "seg_t", "w_branch"]
    return pl.pallas_call(
        _rwkv_post_kernel,
        grid=(b, t // tt),
        in_specs=[tok, tok, tok] + [_full(prm[nm].shape) for nm in names],
        out_specs=pl.BlockSpec((None, tt, D_MODEL), lambda bi, i: (bi, i, 0)),
        out_shape=jax.ShapeDtypeStruct((b, t, D_MODEL), F32),
        compiler_params=_cparams(("parallel", "parallel")),
        name="rwkv_post",
    )(y, bonus, g, *[prm[nm] for nm in names])


def _ssd_kernel(z_ref, xbc_ref, dt_ref, cw_ref, cb_ref, dtb_ref, alog_ref, dskip_ref, ng_ref, ex_ref, tri_ref, wb_ref,
                o_ref, buf, st_ref):
    L = SSD_CHUNK
    first = pl.program_id(1) == 0

    @pl.when(first)
    def _():
        buf[0:SUBLANES, :] = jnp.zeros((SUBLANES, SSD_CONV_CH), F32)
        st_ref[...] = jnp.zeros_like(st_ref)

    buf[SUBLANES:, :] = xbc_ref[...]
    window = buf[...]
    conv = cb_ref[...] + window[SUBLANES:, :] * cw_ref[SSD_CONV - 1:SSD_CONV, :]
    for j in range(SSD_CONV - 1):
        lag = SSD_CONV - 1 - j
        conv = conv + pltpu.roll(window, lag, 0)[SUBLANES:, :] * cw_ref[j:j + 1, :]
    buf[0:SUBLANES, :] = window[L:, :]
    xa = conv * _sigmoid(conv)
    xs = xa[:, :SSD_WIDTH]

    dt = _softplus(dt_ref[...] + dtb_ref[...])
    a = dt * (-jnp.exp(alog_ref[...]))
    row = lax.broadcasted_iota(jnp.int32, (L, L), 0)
    col = lax.broadcasted_iota(jnp.int32, (L, L), 1)
    causal = row >= col
    acs = _pick_dot_left(tri_ref[...], a)
    acs_last = acs[L - 1:L, :]
    stacked = jnp.concatenate(
        [dt, jnp.exp(acs), jnp.exp(acs_last - acs), jnp.broadcast_to(jnp.exp(acs_last), (SUBLANES, LANES))], axis=0)
    wide = _pick_dot(stacked, ex_ref[...])
    dt_x, ea_x, dte_x, el_x = wide[:L], wide[L:2 * L], wide[2 * L:3 * L], wide[3 * L:3 * L + 1]
    acs_t = acs.T
    xdt = xs * dt_x
    lane = lax.broadcasted_iota(jnp.int32, (L, LANES), 1)
    is_lo = lane < SSD_HEAD

    y_parts = []
    for g in range(SSD_GROUPS):
        bg = xa[:, SSD_WIDTH + SSD_STATE * g:SSD_WIDTH + SSD_STATE * (g + 1)]
        cg = xa[:, SSD_WIDTH + SSD_STATE * (SSD_GROUPS + g):SSD_WIDTH + SSD_STATE * (SSD_GROUPS + g + 1)]
        cb = lax.dot_general(cg.astype(BF16), bg.astype(BF16), (((1,), (1,)), ((), ())), preferred_element_type=F32)
        gl = slice(SSD_GW * g, SSD_GW * (g + 1))
        y_off = _bdot(cg, st_ref[g]) * ea_x[:, gl]
        diag = []
        for pp in range(SSD_HG // 2):
            acc = None
            xp = xdt[:, SSD_GW * g + LANES * pp:SSD_GW * g + LANES * (pp + 1)]
            for half in range(2):
                h = SSD_HG * g + 2 * pp + half
                seg = acs[:, h:h + 1] - acs_t[h:h + 1, :]
                lm = jnp.exp(jnp.where(causal, seg, -BIG))
                xh = jnp.where(is_lo if half == 0 else jnp.logical_not(is_lo), xp, 0.0)
                term = _bdot(lm * cb, xh)
                acc = term if acc is None else acc + term
            diag.append(acc)
        y_parts.append(jnp.concatenate(diag, axis=1) + y_off)
        st_ref[g] = st_ref[g] * el_x[:, gl] + _bdot(bg.T, xdt[:, gl] * dte_x[:, gl])
    y = jnp.concatenate(y_parts, axis=1)
    y = y + xs * dskip_ref[...]
    z = z_ref[...]
    y = y * (z * _sigmoid(z))
    normed = []
    for g in range(SSD_GROUPS):
        yg = y[:, SSD_GW * g:SSD_GW * (g + 1)]
        ms = jnp.mean(yg * yg, axis=-1, keepdims=True)
        normed.append(yg * lax.rsqrt(ms + NORM_EPS))
    y = jnp.concatenate(normed, axis=1) * ng_ref[...]
    o_ref[...] = _bdot(y, wb_ref[...])


def _ssd(proj3, prm):
    b, t, _ = proj3.shape
    L = SSD_CHUNK
    tok = lambda w, off: pl.BlockSpec((None, L, w), lambda bi, i: (bi, i, off // w))
    names = ["conv_w", "conv_b", "dt_bias", "a_log", "d_skip", "norm_g", "expand", "tri", "w_branch"]
    return pl.pallas_call(
        _ssd_kernel,
        grid=(b, t // L),
        in_specs=[tok(W_Z, OFF_Z), tok(W_XBC, OFF_XBC), tok(W_DT, OFF_DT)] + [_full(prm[nm].shape) for nm in names],
        out_specs=pl.BlockSpec((None, L, D_MODEL), lambda bi, i: (bi, i, 0)),
        out_shape=jax.ShapeDtypeStruct((b, t, D_MODEL), F32),
        scratch_shapes=[pltpu.VMEM((L + SUBLANES, SSD_CONV_CH), F32), pltpu.VMEM((SSD_GROUPS, SSD_STATE, SSD_GW), F32)],
        compiler_params=_cparams(("parallel", "arbitrary")),
        name="ssd",
    )(proj3, proj3, proj3, *[prm[nm] for nm in names])


def _mix_kernel(ya_ref, yb_ref, gate_ref, h_ref, wo_ref, gf_ref, h_o, ut_o):
    gt = gate_ref[...]
    mixed = _sigmoid(gt[:, :D_MODEL]) * ya_ref[...] + _sigmoid(gt[:, D_MODEL:]) * yb_ref[...]
    h = h_ref[...] + _bdot(mixed, wo_ref[...])
    h_o[...] = h
    ms = jnp.mean(h * h, axis=-1, keepdims=True)
    u = h * lax.rsqrt(ms + NORM_EPS) * gf_ref[...]
    ut_o[...] = pltpu.bitcast(u.T.astype(BF16), jnp.uint32)


def _mix(ya, yb, proj, h, prm, tm):
    n = h.shape[0]
    names = ["w_out", "norm_ffn_g"]
    tok = lambda w, c: pl.BlockSpec((tm, w), lambda i: (i, c))
    return pl.pallas_call(
        _mix_kernel,
        grid=(n // tm,),
        in_specs=[tok(D_MODEL, 0), tok(D_MODEL, 0), tok(W_GATE, OFF_GATE // W_GATE), tok(D_MODEL, 0)]
        + [_full(prm[nm].shape) for nm in names],
        out_specs=[tok(D_MODEL, 0), pl.BlockSpec((D_MODEL // 2, tm), lambda i: (0, i))],
        out_shape=[jax.ShapeDtypeStruct((n, D_MODEL), F32), jax.ShapeDtypeStruct((D_MODEL // 2, n), jnp.uint32)],
        compiler_params=_cparams(("parallel",)),
        name="mix",
    )(ya, yb, proj, h, *[prm[nm] for nm in names])


def _top_rows(x, n, want_rank):
    tm = x.shape[1]
    row_n = lax.broadcasted_iota(jnp.int32, (n, tm), 0)
    tops = jnp.zeros((n, tm), F32)
    rank = jnp.full(x.shape, float(n), F32) if want_rank else None
    m = None
    for k in range(n):
        m = jnp.max(x, axis=0, keepdims=True)
        tops = jnp.where(row_n == k, m, tops)
        hit = x >= m
        if want_rank:
            rank = jnp.where(hit, float(k), rank)
        if k + 1 < n:
            x = jnp.where(hit, -BIG, x)
    return tops, m, rank


def _peer_sel_kernel(ut_ref, wq_ref, keys_ref, e1_o, cnt_o, e2_o, r2_o):
    k = PEER_TOPK
    qt = jnp.dot(wq_ref[...], pltpu.bitcast(ut_ref[...], BF16), preferred_element_type=F32)
    for h in range(PEER_HEADS):
        s = []
        for c in range(2):
            j = 2 * h + c
            s.append(_bdot(keys_ref[j], qt[LANES * j:LANES * (j + 1), :]))
        s1, s2 = s
        top1, thr1, _ = _top_rows(s1, k, False)
        top2, _, rank2 = _top_rows(s2, k, True)
        in1 = s1 >= thr1
        in2 = rank2 < float(k)
        row8 = lax.broadcasted_iota(jnp.int32, (SUBLANES, s1.shape[1]), 0)
        groups = [top1[0:1, :] + top2]
        for k1 in range(1, SUBLANES):
            width = k // (k1 + 1)
            groups.append(jnp.where(row8 < width, top1[k1:k1 + 1, :] + top2[:SUBLANES, :], -BIG))
        groups.append(top1[SUBLANES:, :] + top2[0:1, :])
        x = jnp.concatenate(groups, axis=0)
        m0 = zsum = m = None
        for i in range(k):
            m = jnp.max(x, axis=0, keepdims=True)
            if i == 0:
                m0, zsum = m, jnp.ones_like(m)
            else:
                zsum = zsum + jnp.exp(m - m0)
            if i + 1 < k:
                x = jnp.where(x >= m, -BIG, x)
        thr = m
        cnt_rank = jnp.zeros_like(top1)
        for k2 in range(k):
            cnt_rank = cnt_rank + jnp.where(top1 + top2[k2:k2 + 1, :] >= thr, 1.0, 0.0)
        cnt = jnp.zeros_like(s1)
        for k1 in range(k):
            cnt = jnp.where(s1 == top1[k1:k1 + 1, :], cnt_rank[k1:k1 + 1, :], cnt)
        e1_o[h] = jnp.where(in1, jnp.exp(s1 - top1[0:1, :]), 0.0) / zsum
        cnt_o[h] = cnt
        e2_o[h] = pltpu.bitcast(jnp.where(in2, jnp.exp(s2 - top2[0:1, :]), 0.0).astype(BF16), jnp.uint32)
        r2_o[h] = pltpu.bitcast(rank2.astype(BF16), jnp.uint32)


def _peer_sel(ut, prm, tm):
    n = ut.shape[1]
    shp = (PEER_HEADS, PEER_NKEYS, n)
    ospec = pl.BlockSpec((PEER_HEADS, PEER_NKEYS, tm), lambda i: (0, 0, i))
    pshp = (PEER_HEADS, PEER_NKEYS // 2, n)
    pspec = pl.BlockSpec((PEER_HEADS, PEER_NKEYS // 2, tm), lambda i: (0, 0, i))
    return pl.pallas_call(
        _peer_sel_kernel,
        grid=(n // tm,),
        in_specs=[pl.BlockSpec((D_MODEL // 2, tm), lambda i: (0, i)), _full(prm["wq_t"].shape), _full(prm["keys"].shape)],
        out_specs=[ospec, ospec, pspec, pspec],
        out_shape=[jax.ShapeDtypeStruct(shp, F32), jax.ShapeDtypeStruct(shp, F32),
                   jax.ShapeDtypeStruct(pshp, jnp.uint32), jax.ShapeDtypeStruct(pshp, jnp.uint32)],
        compiler_params=_cparams(("parallel",)),
        name="peer_sel",
    )(ut, prm["wq_t"], prm["keys"])


def _gated_gelu(x, gate):
    c = 0.7978845608028654
    t = jnp.tanh(x * (c + (c * 0.044715) * (x * x)))
    hx = x.astype(BF16) * 0.5
    return (hx + hx * t.astype(BF16)) * gate


def _peer_ffn_kernel(final_g, ut_ref, down_ref, upt_ref, uptp_ref, e1_ref, cnt_ref, e2_ref, r2_ref, h_ref, *rest):
    if final_g:
        gfin_ref, o_ref, *scr = rest
    else:
        o_ref, *scr = rest
    nh = len(scr) // 5
    acc, act, gbuf, hs, tok = (scr[nh * q:nh * (q + 1)] for q in range(5))
    j = pl.program_id(1)
    te, th = act[0].shape
    sub = 2 * SUBLANES

    @pl.when(j == 0)
    def _():
        for hf in range(nh):
            acc[hf][...] = jnp.zeros_like(acc[hf])
            tok[hf][...] = pltpu.bitcast(ut_ref[:, th * hf:th * (hf + 1)], BF16)
        hs[nh - 1][...] = jnp.zeros_like(hs[nh - 1])

    kb = 2
    blk = kb * PEER_NKEYS
    nblk = te // blk
    nm = D_MODEL // blk
    sbt = 4

    def project_down(hf, b):
        rows = slice(blk * b, blk * (b + 1))
        lhs = pltpu.bitcast(down_ref[blk // 2 * b:blk // 2 * (b + 1), :], BF16)
        act[hf][rows, :] = jnp.dot(lhs, tok[hf][...], preferred_element_type=F32)

    def build_gates(hf, b):
        for c in range(th // LANES):
            cl = slice(th * hf + LANES * c, th * hf + LANES * (c + 1))
            first = lambda ref, h, q: jnp.broadcast_to(ref[h, kb * b + q:kb * b + q + 1, cl], (sub, LANES)).astype(BF16)
            cnt = [[first(cnt_ref, h, q) for q in range(kb)] for h in range(PEER_HEADS)]
            e1 = [[first(e1_ref, h, q) for q in range(kb)] for h in range(PEER_HEADS)]
            for s0 in range(0, PEER_NKEYS // sub, sbt):
                gate = [[jnp.zeros((sub, LANES), BF16) for _ in range(sbt)] for _ in range(kb)]
                for h in range(PEER_HEADS):
                    r2 = [pltpu.bitcast(r2_ref[h, SUBLANES * (s0 + s):SUBLANES * (s0 + s + 1), cl], BF16) for s in range(sbt)]
                    e2 = [pltpu.bitcast(e2_ref[h, SUBLANES * (s0 + s):SUBLANES * (s0 + s + 1), cl], BF16) for s in range(sbt)]
                    for q in range(kb):
                        for s in range(sbt):
                            gate[q][s] = gate[q][s] + jnp.where(r2[s] < cnt[h][q], e2[s], 0.0) * e1[h][q]
                for q in range(kb):
                    for s in range(sbt):
                        gw = (blk * b + PEER_NKEYS * q + sub * (s0 + s)) // 2
                        gbuf[hf][gw:gw + SUBLANES, LANES * c:LANES * (c + 1)] = pltpu.bitcast(gate[q][s], jnp.uint32)

    def activate(hf, b):
        for r0 in range(blk * b, blk * (b + 1), sub):
            for c in range(th // LANES):
                cl = slice(LANES * c, LANES * (c + 1))
                gate = pltpu.bitcast(gbuf[hf][r0 // 2:r0 // 2 + SUBLANES, cl], BF16)
                hs[hf][r0:r0 + sub, cl] = _gated_gelu(act[hf][r0:r0 + sub, cl], gate)

    def project_up(w_ref, hf, m):
        rows = slice(blk * m, blk * (m + 1))
        lhs = pltpu.bitcast(w_ref[blk // 2 * m:blk // 2 * (m + 1), :], BF16)
        acc[hf][rows, :] += jnp.dot(lhs, hs[hf][...], preferred_element_type=F32)

    for hf in range(nh):
        for b in range(nblk):
            project_down(hf, b)
            for m in range(b * nm // nblk, (b + 1) * nm // nblk):
                if hf == 0:
                    project_up(uptp_ref, nh - 1, m)
                else:
                    project_up(upt_ref, hf - 1, m)
            build_gates(hf, b)
            if b > 0:
                activate(hf, b - 1)
        activate(hf, nblk - 1)

    @pl.when(j == pl.num_programs(1) - 1)
    def _():
        for m in range(nm):
            project_up(upt_ref, nh - 1, m)
        for hf in range(nh):
            h = h_ref[th * hf:th * (hf + 1), :] + acc[hf][...].T
            if final_g:
                ms = jnp.mean(h * h, axis=-1, keepdims=True)
                h = h * lax.rsqrt(ms + NORM_EPS) * gfin_ref[...]
            o_ref[th * hf:th * (hf + 1), :] = h


def _peer_ffn(ut, down, upt, e1, cnt, e2, r2, h, final_g, tl, te):
    n = h.shape[0]
    nh = 2 if tl % (2 * LANES) == 0 else 1
    sel = pl.BlockSpec((PEER_HEADS, PEER_NKEYS // 2, tl), lambda i, j: (0, 0, i))
    sel1 = pl.BlockSpec((PEER_HEADS, te // PEER_NKEYS, tl), lambda i, j: (0, j, i))
    in_specs = [
        pl.BlockSpec((D_MODEL // 2, tl), lambda i, j: (0, i)),
        pl.BlockSpec((te // 2, D_MODEL), lambda i, j: (j, 0)),
        pl.BlockSpec((D_MODEL // 2, te), lambda i, j: (0, j)),
        pl.BlockSpec((D_MODEL // 2, te), lambda i, j: (0, jnp.maximum(j - 1, 0))),
        sel1, sel1, sel, sel,
        pl.BlockSpec((tl, D_MODEL), lambda i, j: (i, 0)),
    ]
    args = [ut, down, upt, upt, e1, cnt, e2, r2, h]
    if final_g is not None:
        in_specs.append(_full(final_g.shape))
        args.append(final_g)
    return pl.pallas_call(
        functools.partial(_peer_ffn_kernel, final_g is not None),
        grid=(n // tl, PEER_EXPERTS // te),
        in_specs=in_specs,
        out_specs=pl.BlockSpec((tl, D_MODEL), lambda i, j: (i, 0)),
        out_shape=jax.ShapeDtypeStruct((n, D_MODEL), F32),
        scratch_shapes=([pltpu.VMEM((D_MODEL, tl // nh), F32)] * nh + [pltpu.VMEM((te, tl // nh), F32)] * nh
                        + [pltpu.VMEM((te // 2, tl // nh), jnp.uint32)] * nh + [pltpu.VMEM((te, tl // nh), BF16)] * nh
                        + [pltpu.VMEM((D_MODEL, tl // nh), BF16)] * nh),
        compiler_params=_cparams(("parallel", "arbitrary")),
        name="peer_ffn",
    )(*args)


def _pack_weight_kernel(transpose, w_ref, o_ref):
    w = w_ref[...]
    if transpose:
        w = w.T
    o_ref[...] = pltpu.bitcast(w.astype(BF16), jnp.uint32)


def _pack_weight(w_layers, layer, transpose):
    _, r, c = w_layers.shape
    t = D_MODEL
    out_shape, out_map = ((c // 2, r), lambda i, j: (j, i)) if transpose else ((r // 2, c), lambda i, j: (i, j))
    return pl.pallas_call(
        functools.partial(_pack_weight_kernel, transpose),
        grid=(r // t, c // t),
        in_specs=[pl.BlockSpec((None, t, t), lambda i, j: (layer, i, j))],
        out_specs=pl.BlockSpec((t // 2, t), out_map),
        out_shape=jax.ShapeDtypeStruct(out_shape, jnp.uint32),
        compiler_params=_cparams(("parallel", "parallel")),
        name="pack_weight",
    )(w_layers)


def _pad_rows(w, start, total):
    return jnp.zeros((total, w.shape[1]), w.dtype).at[start:start + w.shape[0]].set(w)


def _pad_lanes(v, total):
    return jnp.zeros((1, total), v.dtype).at[0, :v.shape[0]].set(v)


def _row(v):
    return v.reshape(1, -1)


def _constants():
    c = jnp.arange(RW_WIDTH)[:, None] // RW_HEAD == jnp.arange(LANES)[None, :]
    seg = c.astype(BF16)
    expand = (jnp.arange(LANES)[:, None] == jnp.arange(SSD_WIDTH)[None, :] // SSD_HEAD).astype(BF16)
    return seg, seg.T, expand


def kernel(x, norm_mix_g, w_in, rwkv_mu, rwkv_w0, rwkv_w2, rwkv_a0, rwkv_a2, rwkv_g2, rwkv_v0, rwkv_v1, rwkv_v2,
           rwkv_k_k, rwkv_k_a, rwkv_r_k, rwkv_ln_w, rwkv_ln_b, w_rwkv_branch, ssd_conv_w, ssd_conv_b, ssd_dt_bias,
           ssd_A_log, ssd_D, ssd_norm_g, w_ssd_branch, w_out, norm_ffn_g, peer_w_q, peer_sub_keys, peer_down, peer_up,
           norm_final_g):
    bsz, t, d = x.shape
    n = bsz * t
    depth = w_in.shape[0]
    seg, seg_t, expand = _constants()
    h = x.reshape(n, d)
    v_first = None
    o_lora = 3 * RW_WIDTH
    o_z = o_lora + RW_LORA
    o_xbc = o_z + SSD_WIDTH
    o_dt = o_xbc + SSD_CONV_CH
    o_gate = o_dt + SSD_HEADS
    tt = min(TOK_TILE, t)
    step = jnp.arange(tt)
    chunk_tri = ((step[:, None] // REC_CHUNK == step[None, :] // REC_CHUNK) & (step[:, None] >= step[None, :])).astype(BF16)
    chunk_last = (jnp.arange(tt // REC_CHUNK)[:, None] * REC_CHUNK + REC_CHUNK - 1 == step[None, :]).astype(BF16)
    ssd_tri = (jnp.arange(SSD_CHUNK)[:, None] >= jnp.arange(SSD_CHUNK)[None, :]).astype(BF16)
    for l in range(depth):
        wl = w_in[l]
        zpad = lambda k: jnp.zeros((d, k), wl.dtype)
        w_cat = jnp.concatenate([
            wl[:, :o_lora], wl[:, o_xbc:o_dt], wl[:, o_z:o_xbc], wl[:, o_gate:],
            wl[:, o_dt:o_gate], zpad(LANES - SSD_HEADS), wl[:, o_lora:o_z], zpad(RW_LORA_PAD - RW_LORA)], axis=1).astype(BF16)
        proj = _norm_matmul(h, _row(norm_mix_g[l]), w_cat, tm=min(PROJ_TILE[0], n), tn=PROJ_TILE[1])
        proj3 = proj.reshape(bsz, t, PROJ_PAD)

        mu = rwkv_mu[l]
        rw = {
            "mu1": _row(mu[:o_lora]), "mu2": _pad_lanes(mu[o_lora:], RW_LORA_PAD),
            "w0": _row(rwkv_w0[l]), "w2": _pad_rows(rwkv_w2[l], 0, RW_LORA_PAD).astype(BF16),
            "a0": _row(rwkv_a0[l]), "a2": _pad_rows(rwkv_a2[l], DECAY_LORA, RW_LORA_PAD).astype(BF16),
            "g2": _pad_rows(rwkv_g2[l], DECAY_LORA + ICLR_LORA, RW_LORA_PAD).astype(BF16),
            "k_k": _row(rwkv_k_k[l]), "k_a": _row(rwkv_k_a[l]), "r_k": _row(rwkv_r_k[l]),
            "seg": seg, "seg_t": seg_t, "chunk_tri": chunk_tri, "chunk_last": chunk_last,
            "ln_w": _row(rwkv_ln_w[l]), "ln_b": _row(rwkv_ln_b[l]), "w_branch": w_rwkv_branch[l].astype(BF16),
        }
        if l > 0:
            rw["v0"] = _row(rwkv_v0[l - 1])
            rw["v1"] = jnp.zeros((RW_WIDTH, VRES_PAD), F32).at[:, :VRES_LORA].set(rwkv_v1[l - 1]).astype(BF16)
            rw["v2"] = _pad_rows(rwkv_v2[l - 1], 0, VRES_PAD).astype(BF16)
        outs = _rwkv_prep(proj3, v_first, rw, tt)
        at, rt, bt, kt, vb, w_end, g_, bonus = outs[:8]
        if l == 0:
            v_first = outs[8]
        y_r = _rwkv_chunk(at, rt, bt, kt, vb, w_end, tc=tt)
        y_a = _rwkv_post(y_r, bonus, g_, rw, tt)

        sp = {
            "conv_w": ssd_conv_w[l], "conv_b": _row(ssd_conv_b[l]),
            "dt_bias": _pad_lanes(ssd_dt_bias[l], LANES), "a_log": _pad_lanes(ssd_A_log[l], LANES),
            "d_skip": _row(jnp.repeat(ssd_D[l], SSD_HEAD)), "norm_g": _row(ssd_norm_g[l]),
            "expand": expand, "tri": ssd_tri, "w_branch": w_ssd_branch[l].astype(BF16),
        }
        y_b = _ssd(proj3, sp)

        pp = {
            "w_out": w_out[l].astype(BF16), "norm_ffn_g": _row(norm_ffn_g[l]), "wq_t": peer_w_q[l].T.astype(BF16),
            "keys": peer_sub_keys[l].reshape(2 * PEER_HEADS, PEER_NKEYS, PEER_QDIM // 2).astype(BF16),
        }
        h, ut = _mix(y_a.reshape(n, d), y_b.reshape(n, d), proj, h, pp, tm=min(TOK_TILE, n))
        e1, cnt, e2, r2 = _peer_sel(ut, pp, tm=min(TOK_TILE, n))
        fin = _row(norm_final_g) if l == depth - 1 else None
        h = _peer_ffn(ut, _pack_weight(peer_down, l, False), _pack_weight(peer_up, l, True), e1, cnt, e2, r2, h, fin,
                      tl=min(FFN_TILE[0], n), te=FFN_TILE[1])
    return h.reshape(bsz, t, d)
```
